```python
import math
import jax, jax.numpy as jnp
from jax import lax
import numpy as np

D_MODEL = 1024
BATCH = 2
SEQ = 8192
DEPTH = 1
DEC_BATCH = 32
DEC_SEQ = 8
PAST_LEN = 16384
PAGE_SIZE = 128

N_HEADS = 8
HEAD_DIM = D_MODEL // N_HEADS // 2
QK_WIDTH = 2 * N_HEADS * HEAD_DIM
ATTN_WIDTH = N_HEADS * 2 * HEAD_DIM
ROPE_THETA = 10000.0
Q_BLOCK = 128
POOL_WINDOWS = (2, 4, 8, 16)
POOL_GROUPS = len(POOL_WINDOWS)
POOL_WIDTH = D_MODEL // 2
POOL_GC = POOL_WIDTH // POOL_GROUPS
POOL_CTX = max(POOL_WINDOWS) - 1
IN_SPLITS = (POOL_WIDTH, 2 * POOL_WIDTH, 2 * POOL_WIDTH + QK_WIDTH, 2 * POOL_WIDTH + 2 * QK_WIDTH,
             2 * POOL_WIDTH + 2 * QK_WIDTH + ATTN_WIDTH, 2 * POOL_WIDTH + 2 * QK_WIDTH + 2 * ATTN_WIDTH,
             2 * POOL_WIDTH + 2 * QK_WIDTH + 2 * ATTN_WIDTH + D_MODEL)
IN_COLS = 2 * POOL_WIDTH + 2 * QK_WIDTH + 2 * ATTN_WIDTH + 2 * D_MODEL
ALPHA = (2.0 * DEPTH) ** 0.25
BETA = (8.0 * DEPTH) ** -0.25
NORM_EPS = 1e-5

kernel_name = "hybrid_pool_diffattn_decode_step"


def rope(x, pos):
    d = x.shape[-1]
    inv_freq = 1.0 / (ROPE_THETA ** (jnp.arange(0, d, 2, dtype=jnp.float32) / d))
    ang = pos.astype(jnp.float32)[:, None] * inv_freq[None, :]
    cos = jnp.cos(ang)[None, :, None, :]
    sin = jnp.sin(ang)[None, :, None, :]
    xf = x.astype(jnp.float32)
    x1, x2 = xf[..., : d // 2], xf[..., d // 2:]
    return jnp.concatenate([x1 * cos - x2 * sin, x1 * sin + x2 * cos], axis=-1).astype(x.dtype)


def project(x, w_in, pos):
    b, L, _ = x.shape
    h = jnp.einsum('bld,de->ble', x, w_in)
    u_p, z_p, q, k, v, z_a, g_p, g_a = jnp.split(h, IN_SPLITS, axis=-1)
    q = rope(q.reshape(b, L, 2 * N_HEADS, HEAD_DIM), pos) * (HEAD_DIM ** -0.5)
    k = rope(k.reshape(b, L, 2 * N_HEADS, HEAD_DIM), pos)
    v = v.reshape(b, L, N_HEADS, 2 * HEAD_DIM)
    return u_p, z_p, q, k, v, z_a, g_p, g_a


def pool_mix(u, prev, pos, w_mix, scale):
    b, L, P = u.shape
    ext = jnp.concatenate([prev.astype(u.dtype), u], axis=1)
    cs = jnp.cumsum(ext.astype(jnp.float32), axis=1)
    cs = jnp.concatenate([jnp.zeros((b, 1, P), jnp.float32), cs], axis=1)
    start = POOL_CTX + 1
    means = []
    for g, w in enumerate(POOL_WINDOWS):
        sl = slice(g * POOL_GC, (g + 1) * POOL_GC)
        s = cs[:, start:start + L, sl] - cs[:, start - w:start - w + L, sl]
        cnt = jnp.minimum(pos + 1, w).astype(jnp.float32)
        means.append(s / cnt[None, :, None])
    mean = jnp.concatenate(means, axis=-1)
    d = (mean - u.astype(jnp.float32)).reshape(b, L, POOL_GROUPS, POOL_GC)
    mixed = jnp.einsum('blgc,gce->blge', d, w_mix.astype(jnp.float32)).reshape(b, L, P)
    mixed = (mixed * scale.astype(jnp.float32)).astype(u.dtype)
    return mixed, ext[:, -POOL_CTX:]


def diff_attn_core(q, k, v, q_pos, k_pos, lam):
    b, Lq = q.shape[:2]
    Lk = k.shape[1]
    s = jnp.einsum('bqhd,bkhd->bhqk', q, k).astype(jnp.float32)
    mask = k_pos[None, :] <= q_pos[:, None]
    s = jnp.where(mask[None, None], s, -jnp.inf)
    p = jax.nn.softmax(s, axis=-1).reshape(b, N_HEADS, 2, Lq, Lk)
    a = p[:, :, 0] - lam * p[:, :, 1]
    return jnp.einsum('bhqk,bkhe->bqhe', a.astype(v.dtype), v)


def prompt_attn(q, k, v, pos, lam):
    b, L = q.shape[:2]
    nb = L // Q_BLOCK
    qb = jnp.moveaxis(q.reshape(b, nb, Q_BLOCK, 2 * N_HEADS, HEAD_DIM), 1, 0)
    pb = pos.reshape(nb, Q_BLOCK)
    out = lax.map(lambda a: diff_attn_core(a[0], k, v, a[1], pos, lam), (qb, pb))
    return jnp.moveaxis(out, 0, 1).reshape(b, L, N_HEADS, 2 * HEAD_DIM)


def sample_attn(q, k_new, v_new, ck, cv, page_table, lam):
    n_pages = page_table.shape[1]
    past_len = n_pages * ck.shape[1]
    L = q.shape[1]
    k_pos = jnp.arange(past_len + L)
    q_pos = past_len + jnp.arange(L)

    def one(a):
        qs, ks, vs, pt = a
        kp = ck[pt].reshape(past_len, 2 * N_HEADS, HEAD_DIM)
        vp = cv[pt].reshape(past_len, N_HEADS, 2 * HEAD_DIM)
        kk = jnp.concatenate([kp.astype(ks.dtype), ks], axis=0)[None]
        vv = jnp.concatenate([vp.astype(vs.dtype), vs], axis=0)[None]
        return diff_attn_core(qs[None], kk, vv, q_pos, k_pos, lam)[0]

    return lax.map(one, (q, k_new, v_new, page_table))


def finish(x, attn_o, pool_o, z_p, z_a, g_p, g_a, lam_init, subln_w, w_pool_branch,
           w_attn_branch, w_o, ln_g, ln_b):
    b, L, _ = x.shape
    of = attn_o.astype(jnp.float32)
    of = of * lax.rsqrt(jnp.mean(of * of, axis=-1, keepdims=True) + NORM_EPS)
    of = of * subln_w.astype(jnp.float32) * (1.0 - lam_init)
    a = of.reshape(b, L, ATTN_WIDTH).astype(x.dtype) * jax.nn.silu(z_a)
    br_a = jnp.einsum('ble,ed->bld', a, w_attn_branch)
    br_p = jnp.einsum('ble,ed->bld', pool_o * jax.nn.silu(z_p), w_pool_branch)
    m = jax.nn.sigmoid(g_p) * br_p + jax.nn.sigmoid(g_a) * br_a
    out = jnp.einsum('bld,de->ble', m, w_o)
    h = (ALPHA * x.astype(jnp.float32) + out.astype(jnp.float32))
    mu = jnp.mean(h, axis=-1, keepdims=True)
    var = jnp.mean(jnp.square(h - mu), axis=-1, keepdims=True)
    y = (h - mu) * lax.rsqrt(var + NORM_EPS) * ln_g.astype(jnp.float32) + ln_b.astype(jnp.float32)
    return y.astype(x.dtype)


def setup_inputs(seed: int = 0) -> dict:
    key = jax.random.key(seed)
    ks = jax.random.split(key, 20)
    n_pages = PAST_LEN // PAGE_SIZE
    used = DEC_BATCH * n_pages
    n_phys = used + max(1, used // 4)
    nrm = jax.random.normal
    f32 = jnp.float32
    x_prompt = nrm(ks[0], (BATCH, SEQ, D_MODEL), f32)
    x_sample = nrm(ks[1], (DEC_BATCH, DEC_SEQ, D_MODEL), f32)
    cache_k = nrm(ks[2], (DEPTH, n_phys, PAGE_SIZE, 2 * N_HEADS, HEAD_DIM), f32)
    cache_v = nrm(ks[3], (DEPTH, n_phys, PAGE_SIZE, N_HEADS, 2 * HEAD_DIM), f32)
    state_pool = nrm(ks[4], (DEPTH, DEC_BATCH, POOL_CTX, POOL_WIDTH), f32)
    page_table = jax.random.permutation(ks[5], n_phys)[:used].reshape(DEC_BATCH, n_pages).astype(jnp.int32)
    w_in = nrm(ks[6], (DEPTH, D_MODEL, IN_COLS), f32) * D_MODEL ** -0.5
    w_pool_mix = nrm(ks[7], (DEPTH, POOL_GROUPS, POOL_GC, POOL_GC), f32) * POOL_GC ** -0.5
    pool_scale = 1.0 + 0.1 * nrm(ks[8], (DEPTH, POOL_WIDTH), f32)
    lambda_q1 = 0.1 * nrm(ks[9], (DEPTH, HEAD_DIM), f32)
    lambda_k1 = 0.1 * nrm(ks[10], (DEPTH, HEAD_DIM), f32)
    lambda_q2 = 0.1 * nrm(ks[11], (DEPTH, HEAD_DIM), f32)
    lambda_k2 = 0.1 * nrm(ks[12], (DEPTH, HEAD_DIM), f32)
    subln_w = 1.0 + 0.02 * nrm(ks[13], (DEPTH, 2 * HEAD_DIM), f32)
    w_pool_branch = nrm(ks[14], (DEPTH, POOL_WIDTH, D_MODEL), f32) * POOL_WIDTH ** -0.5 * BETA
    w_attn_branch = nrm(ks[15], (DEPTH, ATTN_WIDTH, D_MODEL), f32) * ATTN_WIDTH ** -0.5 * BETA
    w_o = nrm(ks[16], (DEPTH, D_MODEL, D_MODEL), f32) * D_MODEL ** -0.5 * BETA
    ln_g = 1.0 + 0.02 * nrm(ks[17], (DEPTH, D_MODEL), f32)
    ln_b = 0.02 * nrm(ks[18], (DEPTH, D_MODEL), f32)
    return {"x_prompt": x_prompt, "x_sample": x_sample, "cache_k": cache_k, "cache_v": cache_v,
            "state_pool": state_pool, "page_table": page_table, "w_in": w_in,
            "w_pool_mix": w_pool_mix, "pool_scale": pool_scale, "lambda_q1": lambda_q1,
            "lambda_k1": lambda_k1, "lambda_q2": lambda_q2, "lambda_k2": lambda_k2,
            "subln_w": subln_w, "w_pool_branch": w_pool_branch, "w_attn_branch": w_attn_branch,
            "w_o": w_o, "ln_g": ln_g, "ln_b": ln_b}


def reference(x_prompt, x_sample, cache_k, cache_v, state_pool, page_table, w_in, w_pool_mix,
              pool_scale, lambda_q1, lambda_k1, lambda_q2, lambda_k2, subln_w, w_pool_branch,
              w_attn_branch, w_o, ln_g, ln_b):
    past_len = page_table.shape[1] * cache_k.shape[2]
    pos_p = jnp.arange(x_prompt.shape[1])
    pos_s = past_len + jnp.arange(x_sample.shape[1])
    yp, ys = x_prompt, x_sample
    kp_l, vp_l, sp_l, ks_l, vs_l, ss_l = [], [], [], [], [], []
    for l in range(DEPTH):
        lam_init = 0.8 - 0.6 * math.exp(-0.3 * l)
        lam = (jnp.exp(jnp.sum(lambda_q1[l].astype(jnp.float32) * lambda_k1[l].astype(jnp.float32)))
               - jnp.exp(jnp.sum(lambda_q2[l].astype(jnp.float32) * lambda_k2[l].astype(jnp.float32)))
               + lam_init)
        u_p, z_p, q, k, v, z_a, g_p, g_a = project(yp, w_in[l], pos_p)
        zeros_prev = jnp.zeros((yp.shape[0], POOL_CTX, POOL_WIDTH), u_p.dtype)
        pool_o, pst = pool_mix(u_p, zeros_prev, pos_p, w_pool_mix[l], pool_scale[l])
        attn_o = prompt_attn(q, k, v, pos_p, lam)
        yp_next = finish(yp, attn_o, pool_o, z_p, z_a, g_p, g_a, lam_init, subln_w[l],
                         w_pool_branch[l], w_attn_branch[l], w_o[l], ln_g[l], ln_b[l])
        kp_l.append(k); vp_l.append(v); sp_l.append(pst)
        u_s, z_ps, qs, kss, vss, z_as, g_ps, g_as = project(ys, w_in[l], pos_s)
        pool_os, sst = pool_mix(u_s, state_pool[l], pos_s, w_pool_mix[l], pool_scale[l])
        attn_os = sample_attn(qs, kss, vss, cache_k[l], cache_v[l], page_table, lam)
        ys_next = finish(ys, attn_os, pool_os, z_ps, z_as, g_ps, g_as, lam_init, subln_w[l],
                         w_pool_branch[l], w_attn_branch[l], w_o[l], ln_g[l], ln_b[l])
        ks_l.append(kss); vs_l.append(vss); ss_l.append(sst)
        yp, ys = yp_next, ys_next
    return (yp, ys, jnp.stack(kp_l), jnp.stack(vp_l), jnp.stack(sp_l),
            jnp.stack(ks_l), jnp.stack(vs_l), jnp.stack(ss_l))
```

```python
import functools
import math

import jax
import jax.numpy as jnp
from jax import lax
from jax.experimental import pallas as pl
from jax.experimental.pallas import tpu as pltpu

F32 = jnp.float32
BF16 = jnp.bfloat16

N_HEADS = 8
HEAD_DIM = 64
PAIR = 2 * HEAD_DIM
POOL_WINDOWS = (2, 4, 8, 16)
POOL_GC = 128
POOL_WIDTH = POOL_GC * len(POOL_WINDOWS)
POOL_CTX = max(POOL_WINDOWS) - 1
ROPE_THETA = 10000.0
NORM_EPS = 1e-5
DEPTH = 1
ALPHA = (2.0 * DEPTH) ** 0.25
LAM_INIT = 0.8 - 0.6 * math.exp(-0.3 * 0)

C_UP, C_ZP, C_Q, C_K, C_V, C_ZA, C_GP, C_GA, C_END = (0, 512, 1024, 2048, 3072, 4096, 5120, 6144, 7168)

LANES = 128
POOL_HIST = 32
VMEM_LIMIT = 56 * 1024 * 1024

PROJ_TM = 256
ATTN_T = 512
FIN_TM = 512
PAGES_PER_CHUNK = 8


def _silu(z):
    return z * jax.nn.sigmoid(z)


def _rope_block(x, c, sn, sp):
    return x * c + pltpu.roll(x, LANES - 32, axis=1) * sn + pltpu.roll(x, 32, axis=1) * sp


def _rope_tables(pos):
    inv_freq = 1.0 / (ROPE_THETA ** (jnp.arange(0, HEAD_DIM, 2, dtype=F32) / HEAD_DIM))
    ang = pos.astype(F32)[:, None] * inv_freq[None, :]
    cos = jnp.cos(ang)
    sin = jnp.sin(ang)
    z = jnp.zeros_like(sin)
    c = jnp.tile(cos, (1, 4))
    sn = jnp.tile(jnp.concatenate([-sin, z], axis=1), (1, 2))
    sp = jnp.tile(jnp.concatenate([z, sin], axis=1), (1, 2))
    return c, sn, sp


def _pool_mix(sums, u, pos, wmix_ref, pscale_ref):
    outs = []
    for g, w in enumerate(POOL_WINDOWS):
        cols = slice(g * POOL_GC, (g + 1) * POOL_GC)
        cnt = jnp.minimum(pos + 1, w).astype(F32)
        d = sums[g] / cnt - u[:, cols]
        mixed = jnp.dot(d.astype(BF16), wmix_ref[g], preferred_element_type=F32)
        outs.append(mixed * pscale_ref[:, cols])
    return jnp.concatenate(outs, axis=1)


def _proj_prompt_kernel(x_ref, c_ref, sn_ref, sp_ref, w_ref, wmix_ref, pscale_ref, wpb_ref,
                        k_ref, v_ref, qp_ref, kp_ref, vt_ref, sza_ref, sga_ref, mp_ref, tail_ref,
                        ext_ref, sb_ref, *, tm):
    t = pl.program_id(1)
    n = POOL_HIST + tm
    xb = x_ref[0].astype(BF16)

    def proj(lo, hi):
        return jnp.dot(xb, w_ref[:, lo:hi], preferred_element_type=F32)

    u = proj(C_UP, C_ZP)

    @pl.when(t == 0)
    def _():
        ext_ref[0:POOL_HIST, :] = jnp.zeros((POOL_HIST, POOL_WIDTH), F32)

    ext_ref[POOL_HIST:n, :] = u
    sb_ref[0, 8:n, :] = ext_ref[8:n, :] + ext_ref[7:n - 1, :]
    sb_ref[1, 16:n, 128:] = sb_ref[0, 16:n, 128:] + sb_ref[0, 14:n - 2, 128:]
    sb_ref[2, 24:n, 256:] = sb_ref[1, 24:n, 256:] + sb_ref[1, 20:n - 4, 256:]
    s16 = sb_ref[2, 32:n, 384:] + sb_ref[2, 24:n - 8, 384:]
    sums = [sb_ref[0, 32:n, 0:128], sb_ref[1, 32:n, 128:256], sb_ref[2, 32:n, 256:384], s16]
    tail_ref[0] = ext_ref[n - 16:n, :]
    ext_ref[0:POOL_HIST, :] = ext_ref[tm:n, :]

    pos = t * tm + lax.broadcasted_iota(jnp.int32, (tm, POOL_GC), 0)
    pool_o = _pool_mix(sums, u, pos, wmix_ref, pscale_ref)
    zp = proj(C_ZP, C_Q)
    br_p = jnp.dot((pool_o * _silu(zp)).astype(BF16), wpb_ref[...], preferred_element_type=F32)
    mp_ref[0] = jax.nn.sigmoid(proj(C_GP, C_GA)) * br_p

    c = c_ref[...]
    sn = sn_ref[...]
    sp = sp_ref[...]
    q = proj(C_Q, C_K) * (HEAD_DIM ** -0.5)
    for h in range(N_HEADS):
        cols = slice(h * PAIR, (h + 1) * PAIR)
        qp_ref[0, h] = _rope_block(q[:, cols], c, sn, sp).astype(BF16)
    k = proj(C_K, C_V)
    for h in range(N_HEADS):
        cols = slice(h * PAIR, (h + 1) * PAIR)
        kr = _rope_block(k[:, cols], c, sn, sp)
        k_ref[0, :, cols] = kr
        kp_ref[0, h] = kr.astype(BF16)
    v = proj(C_V, C_ZA)
    v_ref[0] = v
    for h in range(N_HEADS):
        vt_ref[0, h, 0] = v[:, h * PAIR:(h + 1) * PAIR].T.astype(BF16)
    sza_ref[0] = _silu(proj(C_ZA, C_GP)).astype(BF16)
    sga_ref[0] = jax.nn.sigmoid(proj(C_GA, C_END)).astype(BF16)


def _proj_prompt(x, tabs, w_in, wmix, pscale, wpb):
    nb, L, d = x.shape
    tm = PROJ_TM
    nt = L // tm
    const = dict(pipeline_mode=pl.Buffered(1))
    row = lambda b, t: (b, t, 0)
    head = lambda b, t: (b, 0, t, 0)
    in_specs = [
        pl.BlockSpec((1, tm, d), row),
        pl.BlockSpec((tm, LANES), lambda b, t: (t, 0)),
        pl.BlockSpec((tm, LANES), lambda b, t: (t, 0)),
        pl.BlockSpec((tm, LANES), lambda b, t: (t, 0)),
        pl.BlockSpec((d, C_END), lambda b, t: (0, 0), **const),
        pl.BlockSpec((len(POOL_WINDOWS), POOL_GC, POOL_GC), lambda b, t: (0, 0, 0), **const),
        pl.BlockSpec((1, POOL_WIDTH), lambda b, t: (0, 0), **const),
        pl.BlockSpec((POOL_WIDTH, d), lambda b, t: (0, 0), **const),
    ]
    out_specs = [
        pl.BlockSpec((1, tm, d), row),
        pl.BlockSpec((1, tm, d), row),
        pl.BlockSpec((1, N_HEADS, tm, PAIR), head),
        pl.BlockSpec((1, N_HEADS, tm, PAIR), head),
        pl.BlockSpec((1, N_HEADS, 1, PAIR, tm), lambda b, t: (b, 0, t, 0, 0)),
        pl.BlockSpec((1, tm, d), row),
        pl.BlockSpec((1, tm, d), row),
        pl.BlockSpec((1, tm, d), row),
        pl.BlockSpec((1, 16, POOL_WIDTH), lambda b, t: (b, 0, 0)),
    ]
    out_shape = [
        jax.ShapeDtypeStruct((nb, L, d), F32),
        jax.ShapeDtypeStruct((nb, L, d), F32),
        jax.ShapeDtypeStruct((nb, N_HEADS, L, PAIR), BF16),
        jax.ShapeDtypeStruct((nb, N_HEADS, L, PAIR), BF16),
        jax.ShapeDtypeStruct((nb, N_HEADS, nt, PAIR, tm), BF16),
        jax.ShapeDtypeStruct((nb, L, d), BF16),
        jax.ShapeDtypeStruct((nb, L, d), BF16),
        jax.ShapeDtypeStruct((nb, L, d), F32),
        jax.ShapeDtypeStruct((nb, 16, POOL_WIDTH), F32),
    ]
    return pl.pallas_call(
        functools.partial(_proj_prompt_kernel, tm=tm),
        grid=(nb, nt),
        in_specs=in_specs,
        out_specs=out_specs,
        out_shape=out_shape,
        scratch_shapes=[pltpu.VMEM((POOL_HIST + tm, POOL_WIDTH), F32),
                        pltpu.VMEM((3, POOL_HIST + tm, POOL_WIDTH), F32)],
        compiler_params=pltpu.CompilerParams(dimension_semantics=("arbitrary", "arbitrary"),
                                             vmem_limit_bytes=VMEM_LIMIT),
        name="proj_prompt",
    )(x, *tabs, w_in, wmix, pscale, wpb)


def _proj_sample_kernel(x_ref, c_ref, sn_ref, sp_ref, prev_ref, w_ref, wmix_ref, pscale_ref, wpb_ref,
                        k_ref, v_ref, q_ref, sza_ref, sga_ref, mp_ref, ext_ref, *, ns, ls, pos0):
    rows = ns * ls
    xb = x_ref[...].astype(BF16)

    def proj(lo, hi):
        return jnp.dot(xb, w_ref[:, lo:hi], preferred_element_type=F32)

    u = proj(C_UP, C_ZP)
    ext_ref[:, 0:16, :] = prev_ref[...]
    ext_ref[:, 16:16 + ls, :] = u.reshape(ns, ls, POOL_WIDTH)
    sums = []
    for g, w in enumerate(POOL_WINDOWS):
        cols = slice(g * POOL_GC, (g + 1) * POOL_GC)
        s = ext_ref[:, 16:16 + ls, cols]
        for j in range(1, w):
            s = s + ext_ref[:, 16 - j:16 + ls - j, cols]
        sums.append(s.reshape(rows, POOL_GC))
    pos = pos0 + lax.broadcasted_iota(jnp.int32, (ns, ls, POOL_GC), 1).reshape(rows, POOL_GC)
    pool_o = _pool_mix(sums, u, pos, wmix_ref, pscale_ref)
    zp = proj(C_ZP, C_Q)
    br_p = jnp.dot((pool_o * _silu(zp)).astype(BF16), wpb_ref[...], preferred_element_type=F32)
    mp_ref[...] = jax.nn.sigmoid(proj(C_GP, C_GA)) * br_p

    c = c_ref[...]
    sn = sn_ref[...]
    sp = sp_ref[...]
    q = proj(C_Q, C_K) * (HEAD_DIM ** -0.5)
    k = proj(C_K, C_V)
    for h in range(N_HEADS):
        cols = slice(h * PAIR, (h + 1) * PAIR)
        q_ref[:, cols] = _rope_block(q[:, cols], c, sn, sp)
        k_ref[:, cols] = _rope_block(k[:, cols], c, sn, sp)
    v_ref[...] = proj(C_V, C_ZA)
    sza_ref[...] = _silu(proj(C_ZA, C_GP))
    sga_ref[...] = jax.nn.sigmoid(proj(C_GA, C_END))


def _proj_sample(x2, tabs, prev16, w_in, wmix, pscale, wpb, ns, ls, pos0):
    rows, d = x2.shape
    big = jax.ShapeDtypeStruct((rows, d), F32)
    return pl.pallas_call(
        functools.partial(_proj_sample_kernel, ns=ns, ls=ls, pos0=pos0),
        out_shape=[big, big, big, big, big, big, jax.ShapeDtypeStruct((ns, 16 + ls, POOL_WIDTH), F32)],
        compiler_params=pltpu.CompilerParams(vmem_limit_bytes=VMEM_LIMIT),
        name="proj_sample",
    )(x2, *tabs, prev16, w_in, wmix, pscale, wpb)


def _attn_tile(kt, vt, w, m, l, acc_ref, mask):
    s = lax.dot_general(kt, w, (((1,), (1,)), ((), ())), preferred_element_type=F32)
    if mask is not None:
        s = jnp.where(mask, s, -jnp.inf)
    m_new = jnp.maximum(m, jnp.max(s, axis=0, keepdims=True))
    alpha = jnp.exp(m - m_new)
    p = jnp.exp(s - m_new)
    l_new = alpha * l + jnp.sum(p, axis=0, keepdims=True)
    pv = jnp.dot(vt, p.astype(BF16), preferred_element_type=F32)
    acc_ref[...] = alpha * acc_ref[...] + pv
    return m_new, l_new


def _attn_prompt_kernel(lam_ref, q_ref, k_ref, vt_ref, sza_ref, subln_ref, out_ref, acc_ref, *, t, cv):
    j = pl.program_id(2)
    q = q_ref[0, 0]
    lane = lax.broadcasted_iota(jnp.int32, (t, PAIR), 1)
    zero = jnp.zeros_like(q)
    w = jnp.concatenate([jnp.where(lane < HEAD_DIM, q, zero), jnp.where(lane >= HEAD_DIM, q, zero)], axis=0)
    acc_ref[...] = jnp.zeros_like(acc_ref)
    nck = t // cv

    def tile(i, m, l, mask):
        kt = k_ref[0, 0, pl.ds(pl.multiple_of(i * t, t), t), :]
        vt = jnp.concatenate([vt_ref[0, 0, i * nck + c] for c in range(nck)], axis=1)
        return _attn_tile(kt, vt, w, m, l, acc_ref, mask)

    m0 = jnp.full((1, 2 * t), -jnp.inf, F32)
    l0 = jnp.zeros((1, 2 * t), F32)
    m, l = lax.fori_loop(0, j, lambda i, ml: tile(i, ml[0], ml[1], None), (m0, l0))
    krow = lax.broadcasted_iota(jnp.int32, (t, 2 * t), 0)
    qcol = lax.broadcasted_iota(jnp.int32, (t, 2 * t), 1)
    qcol = jnp.where(qcol >= t, qcol - t, qcol)
    m, l = tile(j, m, l, krow <= qcol)

    o = acc_ref[...] * (1.0 / l)
    out = o[:, :t] - lam_ref[0, 0] * o[:, t:]
    ms = jnp.mean(out * out, axis=0, keepdims=True)
    out = out * lax.rsqrt(ms + NORM_EPS) * subln_ref[...] * (1.0 - LAM_INIT)
    out_ref[0] = (out.T * sza_ref[0].astype(F32)).astype(BF16)


def _attn_prompt(lam, qp, kp, vt, sza, subln_col):
    nb, nh, L, _ = qp.shape
    t = ATTN_T
    nchunk, cv = vt.shape[2], vt.shape[4]
    return pl.pallas_call(
        functools.partial(_attn_prompt_kernel, t=t, cv=cv),
        grid=(nb, nh, L // t),
        in_specs=[
            pl.BlockSpec(memory_space=pltpu.SMEM),
            pl.BlockSpec((1, 1, t, PAIR), lambda b, h, j: (b, h, j, 0)),
            pl.BlockSpec((1, 1, L, PAIR), lambda b, h, j: (b, h, 0, 0)),
            pl.BlockSpec((1, 1, nchunk, PAIR, cv), lambda b, h, j: (b, h, 0, 0, 0)),
            pl.BlockSpec((1, t, PAIR), lambda b, h, j: (b, j, h)),
            pl.BlockSpec((PAIR, 1), lambda b, h, j: (0, 0)),
        ],
        out_specs=pl.BlockSpec((1, t, PAIR), lambda b, h, j: (b, j, h)),
        out_shape=jax.ShapeDtypeStruct((nb, L, nh * PAIR), BF16),
        scratch_shapes=[pltpu.VMEM((PAIR, 2 * t), F32)],
        compiler_params=pltpu.CompilerParams(dimension_semantics=("arbitrary", "arbitrary", "arbitrary"),
                                             vmem_limit_bytes=VMEM_LIMIT),
        name="attn_prompt",
    )(lam, qp, kp, vt, sza, subln_col)


def _decode_kernel(pt_ref, lam_ref, q_ref, knew_ref, vnew_ref, sza_ref, subln_ref, ck_hbm, cv_hbm, out_ref,
                   kbuf, vbuf, sem, qbd_ref, m_ref, l_ref, acc_ref, *, pages, nc, page_rows):
    s_id = pl.program_id(0)
    c_id = pl.program_id(1)
    step = s_id * nc + c_id
    slot = lax.rem(step, 2)
    total = pl.num_programs(0) * nc
    d = q_ref.shape[2]
    lq = q_ref.shape[1]
    nrow = 2 * N_HEADS * lq
    tc = pages * LANES

    def chunk_copies(seq, chunk, slot_):
        cps = []
        for p in range(pages):
            row0 = pl.multiple_of(pt_ref[seq, chunk * pages + p] * page_rows, page_rows)
            cps.append(pltpu.make_async_copy(ck_hbm.at[pl.ds(row0, page_rows), :], kbuf.at[slot_, p],
                                             sem.at[0, slot_]))
            cps.append(pltpu.make_async_copy(cv_hbm.at[pl.ds(row0, page_rows), :],
                                             vbuf.at[slot_, pl.ds(p * page_rows, page_rows), :],
                                             sem.at[1, slot_]))
        return cps

    @pl.when(step == 0)
    def _():
        for cp in chunk_copies(0, 0, 0):
            cp.start()

    @pl.when(step + 1 < total)
    def _():
        nxt = step + 1
        for cp in chunk_copies(nxt // nc, lax.rem(nxt, nc), 1 - slot):
            cp.start()

    @pl.when(c_id == 0)
    def _():
        qt = jnp.concatenate([q_ref[0]] * (2 * N_HEADS), axis=0)
        rh = lax.broadcasted_iota(jnp.int32, (nrow, d), 0) // lq
        ch = lax.broadcasted_iota(jnp.int32, (nrow, d), 1) // HEAD_DIM
        qbd_ref[...] = jnp.where(rh == ch, qt, 0.0).astype(BF16)
        m_ref[...] = jnp.full_like(m_ref, -jnp.inf)
        l_ref[...] = jnp.zeros_like(l_ref)
        acc_ref[...] = jnp.zeros_like(acc_ref)

    for cp in chunk_copies(s_id, c_id, slot):
        cp.wait()

    def update(s, value_of_head):
        m_prev = m_ref[...]
        m_new = jnp.maximum(m_prev, jnp.max(s, axis=1, keepdims=True))
        alpha = jnp.exp(m_prev - m_new)
        p = jnp.exp(s - m_new)
        l_ref[...] = alpha * l_ref[...] + jnp.sum(p, axis=1, keepdims=True)
        m_ref[...] = m_new
        pb = p.astype(BF16)
        for h in range(N_HEADS):
            rows = slice(2 * lq * h, 2 * lq * (h + 1))
            pv = jnp.dot(pb[rows, :], value_of_head(h), preferred_element_type=F32)
            acc_ref[h] = alpha[rows, :] * acc_ref[h] + pv

    kt = jnp.concatenate([kbuf[slot, p] for p in range(pages)], axis=1).astype(BF16)
    s = jnp.dot(qbd_ref[...], kt, preferred_element_type=F32)
    update(s, lambda h: vbuf[slot, pl.ds(h, tc, stride=N_HEADS), :].astype(BF16))

    @pl.when(c_id == nc - 1)
    def _():
        pad = jnp.zeros((LANES - lq, d), F32)
        kn = jnp.concatenate([knew_ref[0], pad], axis=0).astype(BF16)
        vn = jnp.concatenate([vnew_ref[0], pad], axis=0).astype(BF16)
        s2 = lax.dot_general(qbd_ref[...], kn, (((1,), (1,)), ((), ())), preferred_element_type=F32)
        qi = lax.rem(lax.broadcasted_iota(jnp.int32, (nrow, LANES), 0), lq)
        kj = lax.broadcasted_iota(jnp.int32, (nrow, LANES), 1)
        s2 = jnp.where(kj <= qi, s2, -jnp.inf)
        update(s2, lambda h: vn[:, h * PAIR:(h + 1) * PAIR])
        inv_l = 1.0 / l_ref[...]
        for h in range(N_HEADS):
            rows = slice(2 * lq * h, 2 * lq * (h + 1))
            cols = slice(h * PAIR, (h + 1) * PAIR)
            o = acc_ref[h] * inv_l[rows, :]
            out = o[:lq] - lam_ref[0, 0] * o[lq:]
            ms = jnp.mean(out * out, axis=1, keepdims=True)
            out = out * lax.rsqrt(ms + NORM_EPS) * subln_ref[...] * (1.0 - LAM_INIT)
            out_ref[0, :, cols] = out * sza_ref[0, :, cols]


def _decode_attn(page_table, lam, q3, knew3, vnew3, sza3, subln_row, ck2, cv2, page_rows):
    ns, lq, d = q3.shape
    n_pages = page_table.shape[1]
    pages = PAGES_PER_CHUNK
    nc = n_pages // pages
    nrow = 2 * N_HEADS * lq
    seq = lambda s, c, pt: (s, 0, 0)
    grid_spec = pltpu.PrefetchScalarGridSpec(
        num_scalar_prefetch=1,
        grid=(ns, nc),
        in_specs=[
            pl.BlockSpec(memory_space=pltpu.SMEM),
            pl.BlockSpec((1, lq, d), seq),
            pl.BlockSpec((1, lq, d), seq),
            pl.BlockSpec((1, lq, d), seq),
            pl.BlockSpec((1, lq, d), seq),
            pl.BlockSpec((1, PAIR), lambda s, c, pt: (0, 0)),
            pl.BlockSpec(memory_space=pl.ANY),
            pl.BlockSpec(memory_space=pl.ANY),
        ],
        out_specs=pl.BlockSpec((1, lq, d), seq),
        scratch_shapes=[
            pltpu.VMEM((2, pages, page_rows, LANES), F32),
            pltpu.VMEM((2, pages * page_rows, LANES), F32),
            pltpu.SemaphoreType.DMA((2, 2)),
            pltpu.VMEM((nrow, d), BF16),
            pltpu.VMEM((nrow, 1), F32),
            pltpu.VMEM((nrow, 1), F32),
            pltpu.VMEM((N_HEADS, 2 * lq, PAIR), F32),
        ],
    )
    return pl.pallas_call(
        functools.partial(_decode_kernel, pages=pages, nc=nc, page_rows=page_rows),
        grid_spec=grid_spec,
        out_shape=jax.ShapeDtypeStruct((ns, lq, d), F32),
        compiler_params=pltpu.CompilerParams(dimension_semantics=("arbitrary", "arbitrary"),
                                             vmem_limit_bytes=VMEM_LIMIT),
        name="attn_decode",
    )(page_table, lam, q3, knew3, vnew3, sza3, subln_row, ck2, cv2)


def _finish_kernel(a_ref, sga_ref, mp_ref, x_ref, wab_ref, wo_ref, g_ref, b_ref, y_ref):
    br_a = jnp.dot(a_ref[...].astype(BF16), wab_ref[...], preferred_element_type=F32)
    m = mp_ref[...] + sga_ref[...].astype(F32) * br_a
    out = jnp.dot(m.astype(BF16), wo_ref[...], preferred_element_type=F32)
    h = ALPHA * x_ref[...] + out
    mu = jnp.mean(h, axis=-1, keepdims=True)
    hc = h - mu
    var = jnp.mean(hc * hc, axis=-1, keepdims=True)
    y_ref[...] = hc * lax.rsqrt(var + NORM_EPS) * g_ref[...] + b_ref[...]


def _finish(a, sga, mp, x, wab, wo, ln_g, ln_b):
    rows, d = x.shape
    tm = min(FIN_TM, rows)
    row = lambda i: (i, 0)
    const = lambda i: (0, 0)
    return pl.pallas_call(
        _finish_kernel,
        grid=(rows // tm,),
        in_specs=[pl.BlockSpec((tm, d), row), pl.BlockSpec((tm, d), row), pl.BlockSpec((tm, d), row),
                  pl.BlockSpec((tm, d), row), pl.BlockSpec((d, d), const), pl.BlockSpec((d, d), const),
                  pl.BlockSpec((1, d), const), pl.BlockSpec((1, d), const)],
        out_specs=pl.BlockSpec((tm, d), row),
        out_shape=jax.ShapeDtypeStruct((rows, d), F32),
        compiler_params=pltpu.CompilerParams(dimension_semantics=("arbitrary",), vmem_limit_bytes=VMEM_LIMIT),
        name="finish",
    )(a, sga, mp, x, wab, wo, ln_g, ln_b)


def kernel(x_prompt, x_sample, cache_k, cache_v, state_pool, page_table, w_in, w_pool_mix, pool_scale, lambda_q1, lambda_k1, lambda_q2, lambda_k2, subln_w, w_pool_branch, w_attn_branch, w_o, ln_g, ln_b):
    assert w_in.shape[0] == DEPTH
    nb, L, d = x_prompt.shape
    ns, ls, _ = x_sample.shape
    n_phys, page_size = cache_k.shape[1], cache_k.shape[2]
    assert page_size == LANES
    past_len = page_table.shape[1] * page_size

    lam = (jnp.exp(jnp.sum(lambda_q1[0] * lambda_k1[0])) - jnp.exp(jnp.sum(lambda_q2[0] * lambda_k2[0]))
           + LAM_INIT).reshape(1, 1).astype(F32)
    w_in_b = w_in[0].astype(BF16)
    wmix_b = w_pool_mix[0].astype(BF16)
    wpb_b = w_pool_branch[0].astype(BF16)
    wab_b = w_attn_branch[0].astype(BF16)
    wo_b = w_o[0].astype(BF16)
    pscale = pool_scale[0].reshape(1, POOL_WIDTH)
    g_row = ln_g[0].reshape(1, d)
    b_row = ln_b[0].reshape(1, d)

    tabs_p = _rope_tables(jnp.arange(L))
    k_p, v_p, qp, kp, vt, sza, sga, mp, tail = _proj_prompt(x_prompt, tabs_p, w_in_b, wmix_b, pscale, wpb_b)
    a_p = _attn_prompt(lam, qp, kp, vt, sza, subln_w[0].reshape(PAIR, 1))
    y_p = _finish(a_p.reshape(nb * L, d), sga.reshape(nb * L, d), mp.reshape(nb * L, d),
                  x_prompt.reshape(nb * L, d), wab_b, wo_b, g_row, b_row).reshape(nb, L, d)

    pos_s = jnp.tile(past_len + jnp.arange(ls), ns)
    tabs_s = _rope_tables(pos_s)
    prev16 = jnp.pad(state_pool[0], ((0, 0), (16 - POOL_CTX, 0), (0, 0)))
    x_s2 = x_sample.reshape(ns * ls, d)
    k_s, v_s, q_s, sza_s, sga_s, mp_s, ext_s = _proj_sample(x_s2, tabs_s, prev16, w_in_b, wmix_b, pscale, wpb_b,
                                                            ns, ls, past_len)
    page_rows = page_size * d // LANES
    ck2 = jnp.transpose(cache_k[0], (0, 2, 3, 1)).reshape(n_phys * page_rows, LANES)
    cv2 = cache_v[0].reshape(n_phys * page_rows, LANES)
    r3 = lambda z: z.reshape(ns, ls, d)
    a_s = _decode_attn(page_table, lam, r3(q_s), r3(k_s), r3(v_s), r3(sza_s), subln_w[0].reshape(1, PAIR),
                       ck2, cv2, page_rows)
    y_s = _finish(a_s.reshape(ns * ls, d), sga_s, mp_s, x_s2, wab_b, wo_b, g_row, b_row).reshape(ns, ls, d)

    return (y_p, y_s,
            k_p.reshape(1, nb, L, 2 * N_HEADS, HEAD_DIM), v_p.reshape(1, nb, L, N_HEADS, PAIR),
            tail[:, 16 - POOL_CTX:][None],
            k_s.reshape(1, ns, ls, 2 * N_HEADS, HEAD_DIM), v_s.reshape(1, ns, ls, N_HEADS, PAIR),
            ext_s[:, 16 + ls - POOL_CTX:][None])
```

```python
import functools
import math

import jax
import jax.numpy as jnp
from jax import lax
from jax.experimental import pallas as pl
from jax.experimental.pallas import tpu as pltpu

F32 = jnp.float32
BF16 = jnp.bfloat16

N_HEADS = 8
HEAD_DIM = 64
PAIR = 2 * HEAD_DIM
POOL_WINDOWS = (2, 4, 8, 16)
POOL_GC = 128
POOL_WIDTH = POOL_GC * len(POOL_WINDOWS)
POOL_CTX = max(POOL_WINDOWS) - 1
ROPE_THETA = 10000.0
NORM_EPS = 1e-5
DEPTH = 1
ALPHA = (2.0 * DEPTH) ** 0.25
LAM_INIT = 0.8 - 0.6 * math.exp(-0.3 * 0)
LOG2E = math.log2(math.e)

C_UP, C_ZP, C_Q, C_K, C_V, C_ZA, C_GP, C_GA, C_END = (0, 512, 1024, 2048, 3072, 4096, 5120, 6144, 7168)

LANES = 128
POOL_HIST = 32
VMEM_LIMIT = 56 * 1024 * 1024

PROJ_TM = 256
ATTN_T = 512
ATTN_HPS = 4
FIN_TM = 512
PAGES_PER_CHUNK = 8
DECODE_SLOTS = 3


def _silu(z):
    return z * jax.nn.sigmoid(z)


def _rope_block(x, c, sn, sp):
    return x * c + pltpu.roll(x, LANES - 32, axis=1) * sn + pltpu.roll(x, 32, axis=1) * sp


def _rope_tables(pos):
    inv_freq = 1.0 / (ROPE_THETA ** (jnp.arange(0, HEAD_DIM, 2, dtype=F32) / HEAD_DIM))
    ang = pos.astype(F32)[:, None] * inv_freq[None, :]
    cos = jnp.cos(ang)
    sin = jnp.sin(ang)
    z = jnp.zeros_like(sin)
    c = jnp.tile(cos, (1, 4))
    sn = jnp.tile(jnp.concatenate([-sin, z], axis=1), (1, 2))
    sp = jnp.tile(jnp.concatenate([z, sin], axis=1), (1, 2))
    return c, sn, sp


def _pool_mix(sums, u, pos, wmix_ref, pscale_ref):
    outs = []
    for g, w in enumerate(POOL_WINDOWS):
        cols = slice(g * POOL_GC, (g + 1) * POOL_GC)
        cnt = jnp.minimum(pos + 1, w).astype(F32)
        d = sums[g] / cnt - u[:, cols]
        mixed = jnp.dot(d.astype(BF16), wmix_ref[g], preferred_element_type=F32)
        outs.append(mixed * pscale_ref[:, cols])
    return jnp.concatenate(outs, axis=1)


def _proj_prompt_kernel(x_ref, c_ref, sn_ref, sp_ref, w_ref, wmix_ref, pscale_ref, wpb_ref,
                        k_ref, v_ref, qp_ref, kp_ref, vt_ref, sza_ref, sga_ref, mp_ref, tail_ref,
                        ext_ref, sb_ref, *, tm):
    t = pl.program_id(1)
    n = POOL_HIST + tm
    xb = x_ref[0].astype(BF16)

    def proj(lo, hi):
        return jnp.dot(xb, w_ref[:, lo:hi], preferred_element_type=F32)

    u = proj(C_UP, C_ZP)

    @pl.when(t == 0)
    def _():
        ext_ref[0:POOL_HIST, :] = jnp.zeros((POOL_HIST, POOL_WIDTH), F32)

    ext_ref[POOL_HIST:n, :] = u
    sb_ref[0, 8:n, :] = ext_ref[8:n, :] + ext_ref[7:n - 1, :]
    sb_ref[1, 16:n, 128:] = sb_ref[0, 16:n, 128:] + sb_ref[0, 14:n - 2, 128:]
    sb_ref[2, 24:n, 256:] = sb_ref[1, 24:n, 256:] + sb_ref[1, 20:n - 4, 256:]
    s16 = sb_ref[2, 32:n, 384:] + sb_ref[2, 24:n - 8, 384:]
    sums = [sb_ref[0, 32:n, 0:128], sb_ref[1, 32:n, 128:256], sb_ref[2, 32:n, 256:384], s16]
    tail_ref[0] = ext_ref[n - 16:n, :]
    ext_ref[0:POOL_HIST, :] = ext_ref[tm:n, :]

    pos = t * tm + lax.broadcasted_iota(jnp.int32, (tm, POOL_GC), 0)
    pool_o = _pool_mix(sums, u, pos, wmix_ref, pscale_ref)
    zp = proj(C_ZP, C_Q)
    br_p = jnp.dot((pool_o * _silu(zp)).astype(BF16), wpb_ref[...], preferred_element_type=F32)
    mp_ref[0] = jax.nn.sigmoid(proj(C_GP, C_GA)) * br_p

    c = c_ref[...]
    sn = sn_ref[...]
    sp = sp_ref[...]
    q = proj(C_Q, C_K) * (HEAD_DIM ** -0.5 * LOG2E)
    for h in range(N_HEADS):
        cols = slice(h * PAIR, (h + 1) * PAIR)
        qp_ref[0, h] = _rope_block(q[:, cols], c, sn, sp).astype(BF16)
    k = proj(C_K, C_V)
    for h in range(N_HEADS):
        cols = slice(h * PAIR, (h + 1) * PAIR)
        kr = _rope_block(k[:, cols], c, sn, sp)
        k_ref[0, :, cols] = kr
        kp_ref[0, h] = kr.astype(BF16)
    v = proj(C_V, C_ZA)
    v_ref[0] = v
    for h in range(N_HEADS):
        vt_ref[0, h, 0] = v[:, h * PAIR:(h + 1) * PAIR].T.astype(BF16)
    sza_ref[0] = _silu(proj(C_ZA, C_GP)).astype(BF16)
    sga_ref[0] = jax.nn.sigmoid(proj(C_GA, C_END)).astype(BF16)


def _proj_prompt(x, tabs, w_in, wmix, pscale, wpb):
    nb, L, d = x.shape
    tm = PROJ_TM
    nt = L // tm
    const = dict(pipeline_mode=pl.Buffered(1))
    row = lambda b, t: (b, t, 0)
    head = lambda b, t: (b, 0, t, 0)
    in_specs = [
        pl.BlockSpec((1, tm, d), row),
        pl.BlockSpec((tm, LANES), lambda b, t: (t, 0)),
        pl.BlockSpec((tm, LANES), lambda b, t: (t, 0)),
        pl.BlockSpec((tm, LANES), lambda b, t: (t, 0)),
        pl.BlockSpec((d, C_END), lambda b, t: (0, 0), **const),
        pl.BlockSpec((len(POOL_WINDOWS), POOL_GC, POOL_GC), lambda b, t: (0, 0, 0), **const),
        pl.BlockSpec((1, POOL_WIDTH), lambda b, t: (0, 0), **const),
        pl.BlockSpec((POOL_WIDTH, d), lambda b, t: (0, 0), **const),
    ]
    out_specs = [
        pl.BlockSpec((1, tm, d), row),
        pl.BlockSpec((1, tm, d), row),
        pl.BlockSpec((1, N_HEADS, tm, PAIR), head),
        pl.BlockSpec((1, N_HEADS, tm, PAIR), head),
        pl.BlockSpec((1, N_HEADS, 1, PAIR, tm), lambda b, t: (b, 0, t, 0, 0)),
        pl.BlockSpec((1, tm, d), row),
        pl.BlockSpec((1, tm, d), row),
        pl.BlockSpec((1, tm, d), row),
        pl.BlockSpec((1, 16, POOL_WIDTH), lambda b, t: (b, 0, 0)),
    ]
    out_shape = [
        jax.ShapeDtypeStruct((nb, L, d), F32),
        jax.ShapeDtypeStruct((nb, L, d), F32),
        jax.ShapeDtypeStruct((nb, N_HEADS, L, PAIR), BF16),
        jax.ShapeDtypeStruct((nb, N_HEADS, L, PAIR), BF16),
        jax.ShapeDtypeStruct((nb, N_HEADS, nt, PAIR, tm), BF16),
        jax.ShapeDtypeStruct((nb, L, d), BF16),
        jax.ShapeDtypeStruct((nb, L, d), BF16),
        jax.ShapeDtypeStruct((nb, L, d), F32),
        jax.ShapeDtypeStruct((nb, 16, POOL_WIDTH), F32),
    ]
    return pl.pallas_call(
        functools.partial(_proj_prompt_kernel, tm=tm),
        grid=(nb, nt),
        in_specs=in_specs,
        out_specs=out_specs,
        out_shape=out_shape,
        scratch_shapes=[pltpu.VMEM((POOL_HIST + tm, POOL_WIDTH), F32),
                        pltpu.VMEM((3, POOL_HIST + tm, POOL_WIDTH), F32)],
        compiler_params=pltpu.CompilerParams(dimension_semantics=("arbitrary", "arbitrary"),
                                             vmem_limit_bytes=VMEM_LIMIT),
        name="proj_prompt",
    )(x, *tabs, w_in, wmix, pscale, wpb)


def _proj_sample_kernel(x_ref, c_ref, sn_ref, sp_ref, prev_ref, w_ref, wmix_ref, pscale_ref, wpb_ref,
                        k_ref, v_ref, q_ref, sza_ref, sga_ref, mp_ref, ext_ref, *, ns, ls, pos0):
    rows = ns * ls
    xb = x_ref[...].astype(BF16)

    def proj(lo, hi):
        return jnp.dot(xb, w_ref[:, lo:hi], preferred_element_type=F32)

    u = proj(C_UP, C_ZP)
    ext_ref[:, 0:16, :] = prev_ref[...]
    ext_ref[:, 16:16 + ls, :] = u.reshape(ns, ls, POOL_WIDTH)
    sums = []
    for g, w in enumerate(POOL_WINDOWS):
        cols = slice(g * POOL_GC, (g + 1) * POOL_GC)
        s = ext_ref[:, 16:16 + ls, cols]
        for j in range(1, w):
            s = s + ext_ref[:, 16 - j:16 + ls - j, cols]
        sums.append(s.reshape(rows, POOL_GC))
    pos = pos0 + lax.broadcasted_iota(jnp.int32, (ns, ls, POOL_GC), 1).reshape(rows, POOL_GC)
    pool_o = _pool_mix(sums, u, pos, wmix_ref, pscale_ref)
    zp = proj(C_ZP, C_Q)
    br_p = jnp.dot((pool_o * _silu(zp)).astype(BF16), wpb_ref[...], preferred_element_type=F32)
    mp_ref[...] = jax.nn.sigmoid(proj(C_GP, C_GA)) * br_p

    c = c_ref[...]
    sn = sn_ref[...]
    sp = sp_ref[...]
    q = proj(C_Q, C_K) * (HEAD_DIM ** -0.5)
    k = proj(C_K, C_V)
    for h in range(N_HEADS):
        cols = slice(h * PAIR, (h + 1) * PAIR)
        q_ref[:, cols] = _rope_block(q[:, cols], c, sn, sp)
        k_ref[:, cols] = _rope_block(k[:, cols], c, sn, sp)
    v_ref[...] = proj(C_V, C_ZA)
    sza_ref[...] = _silu(proj(C_ZA, C_GP))
    sga_ref[...] = jax.nn.sigmoid(proj(C_GA, C_END))


def _proj_sample(x2, tabs, prev16, w_in, wmix, pscale, wpb, ns, ls, pos0):
    rows, d = x2.shape
    big = jax.ShapeDtypeStruct((rows, d), F32)
    return pl.pallas_call(
        functools.partial(_proj_sample_kernel, ns=ns, ls=ls, pos0=pos0),
        out_shape=[big, big, big, big, big, big, jax.ShapeDtypeStruct((ns, 16 + ls, POOL_WIDTH), F32)],
        compiler_params=pltpu.CompilerParams(vmem_limit_bytes=VMEM_LIMIT),
        name="proj_sample",
    )(x2, *tabs, prev16, w_in, wmix, pscale, wpb)


COL_REDUCE_ROWS = 64


def _col_reduce(x, op):
    rows, cols = x.shape
    slabs = op(x.reshape(rows // COL_REDUCE_ROWS, COL_REDUCE_ROWS, cols), axis=0)
    return op(slabs, axis=0, keepdims=True)


def _attn_units(kts, vts, ws, ms, ls, acc_ref, mask):
    def scores(u):
        return lax.dot_general(kts[u], ws[u], (((1,), (1,)), ((), ())), preferred_element_type=F32)

    n = len(ws)
    ms_new, ls_new = [], []
    s_next = scores(0)
    for u in range(n):
        s = s_next
        if u + 1 < n:
            s_next = scores(u + 1)
        if mask is not None:
            s = jnp.where(mask, s, -jnp.inf)
        m_new = jnp.maximum(ms[u], _col_reduce(s, jnp.max))
        alpha = jnp.exp2(ms[u] - m_new)
        p = jnp.exp2(s - m_new)
        ls_new.append(alpha * ls[u] + _col_reduce(p, jnp.sum))
        ms_new.append(m_new)
        pv = jnp.dot(vts[u], p.astype(BF16), preferred_element_type=F32)
        acc_ref[u] = alpha * acc_ref[u] + pv
    return tuple(ms_new), tuple(ls_new)


def _attn_prompt_kernel(lam_ref, q_ref, k_ref, vt_ref, sza_ref, subln_ref, out_ref, acc_ref, *, t, cv, hps):
    j = pl.program_id(2)
    lane = lax.broadcasted_iota(jnp.int32, (t, PAIR), 1)
    ws = []
    for u in range(hps):
        q = q_ref[0, u]
        zero = jnp.zeros_like(q)
        ws.append(jnp.concatenate([jnp.where(lane < HEAD_DIM, q, zero), jnp.where(lane >= HEAD_DIM, q, zero)],
                                  axis=0))
    acc_ref[...] = jnp.zeros_like(acc_ref)
    nck = t // cv

    def tile(i, ms, ls, mask):
        kts = [k_ref[0, u, pl.ds(pl.multiple_of(i * t, t), t), :] for u in range(hps)]
        vts = [jnp.concatenate([vt_ref[0, u, i * nck + c] for c in range(nck)], axis=1) for u in range(hps)]
        return _attn_units(kts, vts, ws, ms, ls, acc_ref, mask)

    m0 = tuple(jnp.full((1, 2 * t), -jnp.inf, F32) for _ in range(hps))
    l0 = tuple(jnp.zeros((1, 2 * t), F32) for _ in range(hps))
    ms, ls = lax.fori_loop(0, j, lambda i, c: tile(i, c[0], c[1], None), (m0, l0))
    krow = lax.broadcasted_iota(jnp.int32, (t, 2 * t), 0)
    qcol = lax.broadcasted_iota(jnp.int32, (t, 2 * t), 1)
    qcol = jnp.where(qcol >= t, qcol - t, qcol)
    ms, ls = tile(j, ms, ls, krow <= qcol)

    for u in range(hps):
        o = acc_ref[u] * (1.0 / ls[u])
        out = o[:, :t] - lam_ref[0, 0] * o[:, t:]
        msq = jnp.mean(out * out, axis=0, keepdims=True)
        out = out * lax.rsqrt(msq + NORM_EPS) * subln_ref[...] * (1.0 - LAM_INIT)
        cols = slice(u * PAIR, (u + 1) * PAIR)
        out_ref[0, :, cols] = (out.T * sza_ref[0, :, cols].astype(F32)).astype(BF16)


def _attn_prompt(lam, qp, kp, vt, sza, subln_col):
    nb, nh, L, _ = qp.shape
    t = ATTN_T
    hps = ATTN_HPS
    nchunk, cv = vt.shape[2], vt.shape[4]
    return pl.pallas_call(
        functools.partial(_attn_prompt_kernel, t=t, cv=cv, hps=hps),
        grid=(nb, nh // hps, L // t),
        in_specs=[
            pl.BlockSpec(memory_space=pltpu.SMEM),
            pl.BlockSpec((1, hps, t, PAIR), lambda b, h, j: (b, h, j, 0)),
            pl.BlockSpec((1, hps, L, PAIR), lambda b, h, j: (b, h, 0, 0)),
            pl.BlockSpec((1, hps, nchunk, PAIR, cv), lambda b, h, j: (b, h, 0, 0, 0)),
            pl.BlockSpec((1, t, hps * PAIR), lambda b, h, j: (b, j, h)),
            pl.BlockSpec((PAIR, 1), lambda b, h, j: (0, 0)),
        ],
        out_specs=pl.BlockSpec((1, t, hps * PAIR), lambda b, h, j: (b, j, h)),
        out_shape=jax.ShapeDtypeStruct((nb, L, nh * PAIR), BF16),
        scratch_shapes=[pltpu.VMEM((hps, PAIR, 2 * t), F32)],
        compiler_params=pltpu.CompilerParams(dimension_semantics=("arbitrary", "arbitrary", "arbitrary"),
                                             vmem_limit_bytes=VMEM_LIMIT),
        name="attn_prompt",
    )(lam, qp, kp, vt, sza, subln_col)


def _decode_kernel(pt_ref, lam_ref, q_ref, knew_ref, vnew_ref, sza_ref, subln_ref, ck_hbm, cv_hbm, out_ref,
                   kbuf, vbuf, sem, qbd_ref, m_ref, l_ref, acc_ref, *, pages, nc, page_rows):
    s_id = pl.program_id(0)
    c_id = pl.program_id(1)
    step = s_id * nc + c_id
    slot = lax.rem(step, DECODE_SLOTS)
    total = pl.num_programs(0) * nc
    d = q_ref.shape[2]
    lq = q_ref.shape[1]
    nrow = 2 * N_HEADS * lq
    tc = pages * LANES

    def chunk_copies(seq, chunk, slot_):
        cps = []
        for p in range(pages):
            row0 = pl.multiple_of(pt_ref[seq, chunk * pages + p] * page_rows, page_rows)
            cps.append(pltpu.make_async_copy(ck_hbm.at[pl.ds(row0, page_rows), :], kbuf.at[slot_, p],
                                             sem.at[0, slot_]))
            cps.append(pltpu.make_async_copy(cv_hbm.at[pl.ds(row0, page_rows), :],
                                             vbuf.at[slot_, pl.ds(p * page_rows, page_rows), :],
                                             sem.at[1, slot_]))
        return cps

    @pl.when(step == 0)
    def _():
        for n in range(DECODE_SLOTS - 1):
            for cp in chunk_copies(n // nc, n % nc, n):
                cp.start()

    @pl.when(step + (DECODE_SLOTS - 1) < total)
    def _():
        nxt = step + (DECODE_SLOTS - 1)
        for cp in chunk_copies(nxt // nc, lax.rem(nxt, nc), lax.rem(nxt, DECODE_SLOTS)):
            cp.start()

    @pl.when(c_id == 0)
    def _():
        qt = jnp.concatenate([q_ref[0]] * (2 * N_HEADS), axis=0)
        rh = lax.broadcasted_iota(jnp.int32, (nrow, d), 0) // lq
        ch = lax.broadcasted_iota(jnp.int32, (nrow, d), 1) // HEAD_DIM
        qbd_ref[...] = jnp.where(rh == ch, qt, 0.0).astype(BF16)
        m_ref[...] = jnp.full_like(m_ref, -jnp.inf)
        l_ref[...] = jnp.zeros_like(l_ref)
        acc_ref[...] = jnp.zeros_like(acc_ref)

    for cp in chunk_copies(s_id, c_id, slot):
        cp.wait()

    def update(s, value_of_head):
        m_prev = m_ref[...]
        m_new = jnp.maximum(m_prev, jnp.max(s, axis=1, keepdims=True))
        alpha = jnp.exp(m_prev - m_new)
        p = jnp.exp(s - m_new)
        l_ref[...] = alpha * l_ref[...] + jnp.sum(p, axis=1, keepdims=True)
        m_ref[...] = m_new
        pb = p.astype(BF16)
        for h in range(N_HEADS):
            rows = slice(2 * lq * h, 2 * lq * (h + 1))
            pv = jnp.dot(pb[rows, :], value_of_head(h), preferred_element_type=F32)
            acc_ref[h] = alpha[rows, :] * acc_ref[h] + pv

    kt = jnp.concatenate([kbuf[slot, p] for p in range(pages)], axis=1).astype(BF16)
    s = jnp.dot(qbd_ref[...], kt, preferred_element_type=F32)
    update(s, lambda h: vbuf[slot, pl.ds(h, tc, stride=N_HEADS), :].astype(BF16))

    @pl.when(c_id == nc - 1)
    def _():
        pad = jnp.zeros((LANES - lq, d), F32)
        kn = jnp.concatenate([knew_ref[0], pad], axis=0).astype(BF16)
        vn = jnp.concatenate([vnew_ref[0], pad], axis=0).astype(BF16)
        s2 = lax.dot_general(qbd_ref[...], kn, (((1,), (1,)), ((), ())), preferred_element_type=F32)
        qi = lax.rem(lax.broadcasted_iota(jnp.int32, (nrow, LANES), 0), lq)
        kj = lax.broadcasted_iota(jnp.int32, (nrow, LANES), 1)
        s2 = jnp.where(kj <= qi, s2, -jnp.inf)
        update(s2, lambda h: vn[:, h * PAIR:(h + 1) * PAIR])
        inv_l = 1.0 / l_ref[...]
        for h in range(N_HEADS):
            rows = slice(2 * lq * h, 2 * lq * (h + 1))
            cols = slice(h * PAIR, (h + 1) * PAIR)
            o = acc_ref[h] * inv_l[rows, :]
            out = o[:lq] - lam_ref[0, 0] * o[lq:]
            ms = jnp.mean(out * out, axis=1, keepdims=True)
            out = out * lax.rsqrt(ms + NORM_EPS) * subln_ref[...] * (1.0 - LAM_INIT)
            out_ref[0, :, cols] = out * sza_ref[0, :, cols]


def _decode_attn(page_table, lam, q3, knew3, vnew3, sza3, subln_row, ck2, cv2, page_rows):
    ns, lq, d = q3.shape
    n_pages = page_table.shape[1]
    pages = PAGES_PER_CHUNK
    nc = n_pages // pages
    assert n_pages % pages == 0 and ns * nc >= DECODE_SLOTS - 1
    nrow = 2 * N_HEADS * lq
    seq = lambda s, c, pt: (s, 0, 0)
    grid_spec = pltpu.PrefetchScalarGridSpec(
        num_scalar_prefetch=1,
        grid=(ns, nc),
        in_specs=[
            pl.BlockSpec(memory_space=pltpu.SMEM),
            pl.BlockSpec((1, lq, d), seq),
            pl.BlockSpec((1, lq, d), seq),
            pl.BlockSpec((1, lq, d), seq),
            pl.BlockSpec((1, lq, d), seq),
            pl.BlockSpec((1, PAIR), lambda s, c, pt: (0, 0)),
            pl.BlockSpec(memory_space=pl.ANY),
            pl.BlockSpec(memory_space=pl.ANY),
        ],
        out_specs=pl.BlockSpec((1, lq, d), seq),
        scratch_shapes=[
            pltpu.VMEM((DECODE_SLOTS, pages, page_rows, LANES), F32),
            pltpu.VMEM((DECODE_SLOTS, pages * page_rows, LANES), F32),
            pltpu.SemaphoreType.DMA((2, DECODE_SLOTS)),
            pltpu.VMEM((nrow, d), BF16),
            pltpu.VMEM((nrow, 1), F32),
            pltpu.VMEM((nrow, 1), F32),
            pltpu.VMEM((N_HEADS, 2 * lq, PAIR), F32),
        ],
    )
    return pl.pallas_call(
        functools.partial(_decode_kernel, pages=pages, nc=nc, page_rows=page_rows),
        grid_spec=grid_spec,
        out_shape=jax.ShapeDtypeStruct((ns, lq, d), F32),
        compiler_params=pltpu.CompilerParams(dimension_semantics=("arbitrary", "arbitrary"),
                                             vmem_limit_bytes=VMEM_LIMIT),
        name="attn_decode",
    )(page_table, lam, q3, knew3, vnew3, sza3, subln_row, ck2, cv2)


def _finish_kernel(a_ref, sga_ref, mp_ref, x_ref, wab_ref, wo_ref, g_ref, b_ref, y_ref):
    br_a = jnp.dot(a_ref[...].astype(BF16), wab_ref[...], preferred_element_type=F32)
    m = mp_ref[...] + sga_ref[...].astype(F32) * br_a
    out = jnp.dot(m.astype(BF16), wo_ref[...], preferred_element_type=F32)
    h = ALPHA * x_ref[...] + out
    mu = jnp.mean(h, axis=-1, keepdims=True)
    hc = h - mu
    var = jnp.mean(hc * hc, axis=-1, keepdims=True)
    y_ref[...] = hc * lax.rsqrt(var + NORM_EPS) * g_ref[...] + b_ref[...]


def _finish(a, sga, mp, x, wab, wo, ln_g, ln_b):
    rows, d = x.shape
    tm = min(FIN_TM, rows)
    row = lambda i: (i, 0)
    const = lambda i: (0, 0)
    return pl.pallas_call(
        _finish_kernel,
        grid=(rows // tm,),
        in_specs=[pl.BlockSpec((tm, d), row), pl.BlockSpec((tm, d), row), pl.BlockSpec((tm, d), row),
                  pl.BlockSpec((tm, d), row), pl.BlockSpec((d, d), const), pl.BlockSpec((d, d), const),
                  pl.BlockSpec((1, d), const), pl.BlockSpec((1, d), const)],
        out_specs=pl.BlockSpec((tm, d), row),
        out_shape=jax.ShapeDtypeStruct((rows, d), F32),
        compiler_params=pltpu.CompilerParams(dimension_semantics=("arbitrary",), vmem_limit_bytes=VMEM_LIMIT),
        name="finish",
    )(a, sga, mp, x, wab, wo, ln_g, ln_b)


def kernel(x_prompt, x_sample, cache_k, cache_v, state_pool, page_table, w_in, w_pool_mix, pool_scale, lambda_q1, lambda_k1, lambda_q2, lambda_k2, subln_w, w_pool_branch, w_attn_branch, w_o, ln_g, ln_b):
    assert w_in.shape[0] == DEPTH
    nb, L, d = x_prompt.shape
    ns, ls, _ = x_sample.shape
    n_phys, page_size = cache_k.shape[1], cache_k.shape[2]
    assert page_size == LANES
    past_len = page_table.shape[1] * page_size

    lam = (jnp.exp(jnp.sum(lambda_q1[0] * lambda_k1[0])) - jnp.exp(jnp.sum(lambda_q2[0] * lambda_k2[0]))
           + LAM_INIT).reshape(1, 1).astype(F32)
    w_in_b = w_in[0].astype(BF16)
    wmix_b = w_pool_mix[0].astype(BF16)
    wpb_b = w_pool_branch[0].astype(BF16)
    wab_b = w_attn_branch[0].astype(BF16)
    wo_b = w_o[0].astype(BF16)
    pscale = pool_scale[0].reshape(1, POOL_WIDTH)
    g_row = ln_g[0].reshape(1, d)
    b_row = ln_b[0].reshape(1, d)

    tabs_p = _rope_tables(jnp.arange(L))
    k_p, v_p, qp, kp, vt, sza, sga, mp, tail = _proj_prompt(x_prompt, tabs_p, w_in_b, wmix_b, pscale, wpb_b)
    a_p = _attn_prompt(lam, qp, kp, vt, sza, subln_w[0].reshape(PAIR, 1))
    y_p = _finish(a_p.reshape(nb * L, d), sga.reshape(nb * L, d), mp.reshape(nb * L, d),
                  x_prompt.reshape(nb * L, d), wab_b, wo_b, g_row, b_row).reshape(nb, L, d)

    pos_s = jnp.tile(past_len + jnp.arange(ls), ns)
    tabs_s = _rope_tables(pos_s)
    prev16 = jnp.pad(state_pool[0], ((0, 0), (16 - POOL_CTX, 0), (0, 0)))
    x_s2 = x_sample.reshape(ns * ls, d)
    k_s, v_s, q_s, sza_s, sga_s, mp_s, ext_s = _proj_sample(x_s2, tabs_s, prev16, w_in_b, wmix_b, pscale, wpb_b,
                                                            ns, ls, past_len)
    page_rows = page_size * d // LANES
    ck2 = jnp.transpose(cache_k[0], (0, 2, 3, 1)).reshape(n_phys * page_rows, LANES)
    cv2 = cache_v[0].reshape(n_phys * page_rows, LANES)
    r3 = lambda z: z.reshape(ns, ls, d)
    a_s = _decode_attn(page_table, lam, r3(q_s), r3(k_s), r3(v_s), r3(sza_s), subln_w[0].reshape(1, PAIR),
                       ck2, cv2, page_rows)
    y_s = _finish(a_s.reshape(ns * ls, d), sga_s, mp_s, x_s2, wab_b, wo_b, g_row, b_row).reshape(ns, ls, d)

    return (y_p, y_s,
            k_p.reshape(1, nb, L, 2 * N_HEADS, HEAD_DIM), v_p.reshape(1, nb, L, N_HEADS, PAIR),
            tail[:, 16 - POOL_CTX:][None],
            k_s.reshape(1, ns, ls, 2 * N_HEADS, HEAD_DIM), v_s.reshape(1, ns, ls, N_HEADS, PAIR),
            ext_s[:, 16 + ls - POOL_CTX:][None])
```

```python
import functools
import math

import jax
import jax.numpy as jnp
from jax import lax
from jax.experimental import pallas as pl
from jax.experimental.pallas import tpu as pltpu

F32 = jnp.float32
BF16 = jnp.bfloat16

N_HEADS = 8
HEAD_DIM = 64
PAIR = 2 * HEAD_DIM
POOL_WINDOWS = (2, 4, 8, 16)
POOL_GC = 128
POOL_WIDTH = POOL_GC * len(POOL_WINDOWS)
POOL_CTX = max(POOL_WINDOWS) - 1
ROPE_THETA = 10000.0
NORM_EPS = 1e-5
DEPTH = 1
ALPHA = (2.0 * DEPTH) ** 0.25
LAM_INIT = 0.8 - 0.6 * math.exp(-0.3 * 0)
LOG2E = math.log2(math.e)

C_UP, C_ZP, C_Q, C_K, C_V, C_ZA, C_GP, C_GA, C_END = (0, 512, 1024, 2048, 3072, 4096, 5120, 6144, 7168)

LANES = 128
POOL_HIST = 32
VMEM_LIMIT = 56 * 1024 * 1024

PROJ_TM = 256
ATTN_T = 512
ATTN_HPS = 4
FIN_TM = 512
PAGES_PER_CHUNK = 8


def _silu(z):
    return z * jax.nn.sigmoid(z)


def _rope_block(x, c, sn, sp):
    return x * c + pltpu.roll(x, LANES - 32, axis=1) * sn + pltpu.roll(x, 32, axis=1) * sp


def _rope_tables(pos):
    inv_freq = 1.0 / (ROPE_THETA ** (jnp.arange(0, HEAD_DIM, 2, dtype=F32) / HEAD_DIM))
    ang = pos.astype(F32)[:, None] * inv_freq[None, :]
    cos = jnp.cos(ang)
    sin = jnp.sin(ang)
    z = jnp.zeros_like(sin)
    c = jnp.tile(cos, (1, 4))
    sn = jnp.tile(jnp.concatenate([-sin, z], axis=1), (1, 2))
    sp = jnp.tile(jnp.concatenate([z, sin], axis=1), (1, 2))
    return c, sn, sp


def _pool_mix(sums, u, pos, wmix_ref, pscale_ref):
    outs = []
    for g, w in enumerate(POOL_WINDOWS):
        cols = slice(g * POOL_GC, (g + 1) * POOL_GC)
        cnt = jnp.minimum(pos + 1, w).astype(F32)
        d = sums[g] / cnt - u[:, cols]
        mixed = jnp.dot(d.astype(BF16), wmix_ref[g], preferred_element_type=F32)
        outs.append(mixed * pscale_ref[:, cols])
    return jnp.concatenate(outs, axis=1)


def _proj_prompt_kernel(x_ref, c_ref, sn_ref, sp_ref, w_ref, wmix_ref, pscale_ref, wpb_ref,
                        k_ref, v_ref, qp_ref, kp_ref, vt_ref, sza_ref, sga_ref, mp_ref, tail_ref,
                        ext_ref, sb_ref, *, tm):
    t = pl.program_id(1)
    n = POOL_HIST + tm
    xb = x_ref[0].astype(BF16)

    def proj(lo, hi):
        return jnp.dot(xb, w_ref[:, lo:hi], preferred_element_type=F32)

    u = proj(C_UP, C_ZP)

    @pl.when(t == 0)
    def _():
        ext_ref[0:POOL_HIST, :] = jnp.zeros((POOL_HIST, POOL_WIDTH), F32)

    ext_ref[POOL_HIST:n, :] = u
    sb_ref[0, 8:n, :] = ext_ref[8:n, :] + ext_ref[7:n - 1, :]
    sb_ref[1, 16:n, 128:] = sb_ref[0, 16:n, 128:] + sb_ref[0, 14:n - 2, 128:]
    sb_ref[2, 24:n, 256:] = sb_ref[1, 24:n, 256:] + sb_ref[1, 20:n - 4, 256:]
    s16 = sb_ref[2, 32:n, 384:] + sb_ref[2, 24:n - 8, 384:]
    sums = [sb_ref[0, 32:n, 0:128], sb_ref[1, 32:n, 128:256], sb_ref[2, 32:n, 256:384], s16]
    tail_ref[0] = ext_ref[n - 16:n, :]
    ext_ref[0:POOL_HIST, :] = ext_ref[tm:n, :]

    pos = t * tm + lax.broadcasted_iota(jnp.int32, (tm, POOL_GC), 0)
    pool_o = _pool_mix(sums, u, pos, wmix_ref, pscale_ref)
    zp = proj(C_ZP, C_Q)
    br_p = jnp.dot((pool_o * _silu(zp)).astype(BF16), wpb_ref[...], preferred_element_type=F32)
    mp_ref[0] = jax.nn.sigmoid(proj(C_GP, C_GA)) * br_p

    c = c_ref[...]
    sn = sn_ref[...]
    sp = sp_ref[...]
    q = proj(C_Q, C_K) * (HEAD_DIM ** -0.5 * LOG2E)
    for h in range(N_HEADS):
        cols = slice(h * PAIR, (h + 1) * PAIR)
        qp_ref[0, h] = _rope_block(q[:, cols], c, sn, sp).astype(BF16)
    k = proj(C_K, C_V)
    for h in range(N_HEADS):
        cols = slice(h * PAIR, (h + 1) * PAIR)
        kr = _rope_block(k[:, cols], c, sn, sp)
        k_ref[0, :, cols] = kr
        kp_ref[0, h] = kr.astype(BF16)
    v = proj(C_V, C_ZA)
    v_ref[0] = v
    for h in range(N_HEADS):
        vt_ref[0, h, 0] = v[:, h * PAIR:(h + 1) * PAIR].T.astype(BF16)
    sza_ref[0] = _silu(proj(C_ZA, C_GP)).astype(BF16)
    sga_ref[0] = jax.nn.sigmoid(proj(C_GA, C_END)).astype(BF16)


def _proj_prompt(x, tabs, w_in, wmix, pscale, wpb):
    nb, L, d = x.shape
    tm = PROJ_TM
    nt = L // tm
    const = dict(pipeline_mode=pl.Buffered(1))
    row = lambda b, t: (b, t, 0)
    head = lambda b, t: (b, 0, t, 0)
    in_specs = [
        pl.BlockSpec((1, tm, d), row),
        pl.BlockSpec((tm, LANES), lambda b, t: (t, 0)),
        pl.BlockSpec((tm, LANES), lambda b, t: (t, 0)),
        pl.BlockSpec((tm, LANES), lambda b, t: (t, 0)),
        pl.BlockSpec((d, C_END), lambda b, t: (0, 0), **const),
        pl.BlockSpec((len(POOL_WINDOWS), POOL_GC, POOL_GC), lambda b, t: (0, 0, 0), **const),
        pl.BlockSpec((1, POOL_WIDTH), lambda b, t: (0, 0), **const),
        pl.BlockSpec((POOL_WIDTH, d), lambda b, t: (0, 0), **const),
    ]
    out_specs = [
        pl.BlockSpec((1, tm, d), row),
        pl.BlockSpec((1, tm, d), row),
        pl.BlockSpec((1, N_HEADS, tm, PAIR), head),
        pl.BlockSpec((1, N_HEADS, tm, PAIR), head),
        pl.BlockSpec((1, N_HEADS, 1, PAIR, tm), lambda b, t: (b, 0, t, 0, 0)),
        pl.BlockSpec((1, tm, d), row),
        pl.BlockSpec((1, tm, d), row),
        pl.BlockSpec((1, tm, d), row),
        pl.BlockSpec((1, 16, POOL_WIDTH), lambda b, t: (b, 0, 0)),
    ]
    out_shape = [
        jax.ShapeDtypeStruct((nb, L, d), F32),
        jax.ShapeDtypeStruct((nb, L, d), F32),
        jax.ShapeDtypeStruct((nb, N_HEADS, L, PAIR), BF16),
        jax.ShapeDtypeStruct((nb, N_HEADS, L, PAIR), BF16),
        jax.ShapeDtypeStruct((nb, N_HEADS, nt, PAIR, tm), BF16),
        jax.ShapeDtypeStruct((nb, L, d), BF16),
        jax.ShapeDtypeStruct((nb, L, d), BF16),
        jax.ShapeDtypeStruct((nb, L, d), F32),
        jax.ShapeDtypeStruct((nb, 16, POOL_WIDTH), F32),
    ]
    return pl.pallas_call(
        functools.partial(_proj_prompt_kernel, tm=tm),
        grid=(nb, nt),
        in_specs=in_specs,
        out_specs=out_specs,
        out_shape=out_shape,
        scratch_shapes=[pltpu.VMEM((POOL_HIST + tm, POOL_WIDTH), F32),
                        pltpu.VMEM((3, POOL_HIST + tm, POOL_WIDTH), F32)],
        compiler_params=pltpu.CompilerParams(dimension_semantics=("arbitrary", "arbitrary"),
                                             vmem_limit_bytes=VMEM_LIMIT),
        name="proj_prompt",
    )(x, *tabs, w_in, wmix, pscale, wpb)


def _proj_sample_kernel(x_ref, c_ref, sn_ref, sp_ref, prev_ref, w_ref, wmix_ref, pscale_ref, wpb_ref,
                        k_ref, v_ref, q_ref, sza_ref, sga_ref, mp_ref, ext_ref, *, ns, ls, pos0):
    rows = ns * ls
    xb = x_ref[...].astype(BF16)

    def proj(lo, hi):
        return jnp.dot(xb, w_ref[:, lo:hi], preferred_element_type=F32)

    u = proj(C_UP, C_ZP)
    ext_ref[:, 0:16, :] = prev_ref[...]
    ext_ref[:, 16:16 + ls, :] = u.reshape(ns, ls, POOL_WIDTH)
    sums = []
    for g, w in enumerate(POOL_WINDOWS):
        cols = slice(g * POOL_GC, (g + 1) * POOL_GC)
        s = ext_ref[:, 16:16 + ls, cols]
        for j in range(1, w):
            s = s + ext_ref[:, 16 - j:16 + ls - j, cols]
        sums.append(s.reshape(rows, POOL_GC))
    pos = pos0 + lax.broadcasted_iota(jnp.int32, (ns, ls, POOL_GC), 1).reshape(rows, POOL_GC)
    pool_o = _pool_mix(sums, u, pos, wmix_ref, pscale_ref)
    zp = proj(C_ZP, C_Q)
    br_p = jnp.dot((pool_o * _silu(zp)).astype(BF16), wpb_ref[...], preferred_element_type=F32)
    mp_ref[...] = jax.nn.sigmoid(proj(C_GP, C_GA)) * br_p

    c = c_ref[...]
    sn = sn_ref[...]
    sp = sp_ref[...]
    q = proj(C_Q, C_K) * (HEAD_DIM ** -0.5)
    k = proj(C_K, C_V)
    for h in range(N_HEADS):
        cols = slice(h * PAIR, (h + 1) * PAIR)
        q_ref[:, cols] = _rope_block(q[:, cols], c, sn, sp)
        k_ref[:, cols] = _rope_block(k[:, cols], c, sn, sp)
    v_ref[...] = proj(C_V, C_ZA)
    sza_ref[...] = _silu(proj(C_ZA, C_GP))
    sga_ref[...] = jax.nn.sigmoid(proj(C_GA, C_END))


def _proj_sample(x2, tabs, prev16, w_in, wmix, pscale, wpb, ns, ls, pos0):
    rows, d = x2.shape
    big = jax.ShapeDtypeStruct((rows, d), F32)
    return pl.pallas_call(
        functools.partial(_proj_sample_kernel, ns=ns, ls=ls, pos0=pos0),
        out_shape=[big, big, big, big, big, big, jax.ShapeDtypeStruct((ns, 16 + ls, POOL_WIDTH), F32)],
        compiler_params=pltpu.CompilerParams(vmem_limit_bytes=VMEM_LIMIT),
        name="proj_sample",
    )(x2, *tabs, prev16, w_in, wmix, pscale, wpb)


COL_REDUCE_ROWS = 64


def _col_reduce(x, op):
    rows, cols = x.shape
    slabs = op(x.reshape(rows // COL_REDUCE_ROWS, COL_REDUCE_ROWS, cols), axis=0)
    return op(slabs, axis=0, keepdims=True)


def _attn_units(kts, vts, ws, ms, ls, acc_ref, mask):
    def scores(u):
        return lax.dot_general(kts[u], ws[u], (((1,), (1,)), ((), ())), preferred_element_type=F32)

    n = len(ws)
    ms_new, ls_new = [], []
    s_next = scores(0)
    for u in range(n):
        s = s_next
        if u + 1 < n:
            s_next = scores(u + 1)
        if mask is not None:
            s = jnp.where(mask, s, -jnp.inf)
        m_new = jnp.maximum(ms[u], _col_reduce(s, jnp.max))
        alpha = jnp.exp2(ms[u] - m_new)
        p = jnp.exp2(s - m_new)
        ls_new.append(alpha * ls[u] + _col_reduce(p, jnp.sum))
        ms_new.append(m_new)
        pv = jnp.dot(vts[u], p.astype(BF16), preferred_element_type=F32)
        acc_ref[u] = alpha * acc_ref[u] + pv
    return tuple(ms_new), tuple(ls_new)


def _attn_fused_kernel(pt_ref, lam_ref, q_ref, k_ref, vt_ref, sza_ref, subln_ref,
                       qs_ref, knew_ref, vnew_ref, szas_ref, sublnr_ref, ck_hbm, cv_hbm,
                       out_ref, dout_ref,
                       acc_ref, kbuf, vbuf, sem, qbd_ref, dm_ref, dl_ref, dacc_ref, gcount_ref,
                       *, t, cv, hps, pages, nc, page_rows, total):
    j = pl.program_id(2)
    lin = (pl.program_id(0) * pl.num_programs(1) + pl.program_id(1)) * pl.num_programs(2) + j
    n_lin = pl.num_programs(0) * pl.num_programs(1) * pl.num_programs(2)
    dma_refs = (pt_ref, ck_hbm, cv_hbm, kbuf, vbuf, sem)
    dec_kw = dict(pages=pages, nc=nc, page_rows=page_rows)

    @pl.when(lin == 0)
    def _():
        gcount_ref[0] = 0
        for cp in _decode_copies(dma_refs, 0, 0, **dec_kw):
            cp.start()

    g0 = gcount_ref[0]

    def decode(g):
        @pl.when(g < total)
        def _():
            _decode_step(g, dma_refs, lam_ref, qs_ref, knew_ref, vnew_ref, szas_ref, sublnr_ref, dout_ref,
                         qbd_ref, dm_ref, dl_ref, dacc_ref, total=total, **dec_kw)

    lane = lax.broadcasted_iota(jnp.int32, (t, PAIR), 1)
    ws = []
    for u in range(hps):
        q = q_ref[0, u]
        zero = jnp.zeros_like(q)
        ws.append(jnp.concatenate([jnp.where(lane < HEAD_DIM, q, zero), jnp.where(lane >= HEAD_DIM, q, zero)],
                                  axis=0))
    acc_ref[...] = jnp.zeros_like(acc_ref)
    nck = t // cv

    def tile(i, ms, ls, mask):
        decode(g0 + i)
        kts = [k_ref[0, u, pl.ds(pl.multiple_of(i * t, t), t), :] for u in range(hps)]
        vts = [jnp.concatenate([vt_ref[0, u, i * nck + c] for c in range(nck)], axis=1) for u in range(hps)]
        return _attn_units(kts, vts, ws, ms, ls, acc_ref, mask)

    m0 = tuple(jnp.full((1, 2 * t), -jnp.inf, F32) for _ in range(hps))
    l0 = tuple(jnp.zeros((1, 2 * t), F32) for _ in range(hps))
    ms, ls = lax.fori_loop(0, j, lambda i, c: tile(i, c[0], c[1], None), (m0, l0))
    krow = lax.broadcasted_iota(jnp.int32, (t, 2 * t), 0)
    qcol = lax.broadcasted_iota(jnp.int32, (t, 2 * t), 1)
    qcol = jnp.where(qcol >= t, qcol - t, qcol)
    ms, ls = tile(j, ms, ls, krow <= qcol)

    for u in range(hps):
        o = acc_ref[u] * (1.0 / ls[u])
        out = o[:, :t] - lam_ref[0, 0] * o[:, t:]
        msq = jnp.mean(out * out, axis=0, keepdims=True)
        out = out * lax.rsqrt(msq + NORM_EPS) * subln_ref[...] * (1.0 - LAM_INIT)
        cols = slice(u * PAIR, (u + 1) * PAIR)
        out_ref[0, :, cols] = (out.T * sza_ref[0, :, cols].astype(F32)).astype(BF16)

    g_next = g0 + j + 1
    gcount_ref[0] = g_next

    @pl.when(lin == n_lin - 1)
    def _():
        def body(g, carry):
            decode(g)
            return carry
        lax.fori_loop(g_next, total, body, 0)


def _attn_fused(page_table, lam, qp, kp, vt, sza, subln_col, q3, knew3, vnew3, sza3, subln_row, ck2, cv2,
                page_rows):
    nb, nh, L, _ = qp.shape
    ns, lq, d = q3.shape
    t = ATTN_T
    hps = ATTN_HPS
    nchunk, cv = vt.shape[2], vt.shape[4]
    n_pages = page_table.shape[1]
    pages = PAGES_PER_CHUNK
    assert n_pages % pages == 0
    nc = n_pages // pages
    nrow = 2 * N_HEADS * lq
    once = dict(pipeline_mode=pl.Buffered(1))
    whole = lambda b, h, j, pt: (0, 0, 0)
    grid_spec = pltpu.PrefetchScalarGridSpec(
        num_scalar_prefetch=1,
        grid=(nb, nh // hps, L // t),
        in_specs=[
            pl.BlockSpec(memory_space=pltpu.SMEM),
            pl.BlockSpec((1, hps, t, PAIR), lambda b, h, j, pt: (b, h, j, 0)),
            pl.BlockSpec((1, hps, L, PAIR), lambda b, h, j, pt: (b, h, 0, 0), **once),
            pl.BlockSpec((1, hps, nchunk, PAIR, cv), lambda b, h, j, pt: (b, h, 0, 0, 0), **once),
            pl.BlockSpec((1, t, hps * PAIR), lambda b, h, j, pt: (b, j, h)),
            pl.BlockSpec((PAIR, 1), lambda b, h, j, pt: (0, 0)),
            pl.BlockSpec((ns, lq, d), whole, **once),
            pl.BlockSpec((ns, lq, d), whole, **once),
            pl.BlockSpec((ns, lq, d), whole, **once),
            pl.BlockSpec((ns, lq, d), whole, **once),
            pl.BlockSpec((1, PAIR), lambda b, h, j, pt: (0, 0)),
            pl.BlockSpec(memory_space=pl.ANY),
            pl.BlockSpec(memory_space=pl.ANY),
        ],
        out_specs=[
            pl.BlockSpec((1, t, hps * PAIR), lambda b, h, j, pt: (b, j, h)),
            pl.BlockSpec((ns, lq, d), whole),
        ],
        scratch_shapes=[
            pltpu.VMEM((hps, PAIR, 2 * t), F32),
            pltpu.VMEM((2, pages, page_rows, LANES), F32),
            pltpu.VMEM((2, pages * page_rows, LANES), F32),
            pltpu.SemaphoreType.DMA((2, 2)),
            pltpu.VMEM((nrow, d), BF16),
            pltpu.VMEM((nrow, 1), F32),
            pltpu.VMEM((nrow, 1), F32),
            pltpu.VMEM((N_HEADS, 2 * lq, PAIR), F32),
            pltpu.SMEM((1,), jnp.int32),
        ],
    )
    return pl.pallas_call(
        functools.partial(_attn_fused_kernel, t=t, cv=cv, hps=hps, pages=pages, nc=nc, page_rows=page_rows,
                          total=ns * nc),
        grid_spec=grid_spec,
        out_shape=[jax.ShapeDtypeStruct((nb, L, nh * PAIR), BF16), jax.ShapeDtypeStruct((ns, lq, d), F32)],
        compiler_params=pltpu.CompilerParams(dimension_semantics=("arbitrary", "arbitrary", "arbitrary"),
                                             vmem_limit_bytes=VMEM_LIMIT),
        name="attn_fused",
    )(page_table, lam, qp, kp, vt, sza, subln_col, q3, knew3, vnew3, sza3, subln_row, ck2, cv2)


def _divmod(g, n):
    if isinstance(g, int):
        return divmod(g, n)
    return lax.div(g, jnp.int32(n)), lax.rem(g, jnp.int32(n))


def _decode_copies(refs, g, slot, *, pages, nc, page_rows):
    pt_ref, ck_hbm, cv_hbm, kbuf, vbuf, sem = refs
    seq, chunk = _divmod(g, nc)
    cps = []
    for p in range(pages):
        row0 = pl.multiple_of(pt_ref[seq, chunk * pages + p] * page_rows, page_rows)
        cps.append(pltpu.make_async_copy(ck_hbm.at[pl.ds(row0, page_rows), :], kbuf.at[slot, p], sem.at[0, slot]))
        cps.append(pltpu.make_async_copy(cv_hbm.at[pl.ds(row0, page_rows), :],
                                         vbuf.at[slot, pl.ds(p * page_rows, page_rows), :], sem.at[1, slot]))
    return cps


def _decode_step(g, dma_refs, lam_ref, q_ref, knew_ref, vnew_ref, sza_ref, subln_ref, out_ref,
                 qbd_ref, m_ref, l_ref, acc_ref, *, pages, nc, page_rows, total):
    kbuf, vbuf = dma_refs[3], dma_refs[4]
    lq, d = q_ref.shape[1], q_ref.shape[2]
    nrow = 2 * N_HEADS * lq
    tc = pages * LANES
    seq, c_id = _divmod(g, nc)
    slot = lax.rem(g, 2)
    copies = functools.partial(_decode_copies, dma_refs, pages=pages, nc=nc, page_rows=page_rows)

    @pl.when(g + 1 < total)
    def _():
        for cp in copies(g + 1, 1 - slot):
            cp.start()

    @pl.when(c_id == 0)
    def _():
        qt = jnp.concatenate([q_ref[seq]] * (2 * N_HEADS), axis=0)
        rh = lax.broadcasted_iota(jnp.int32, (nrow, d), 0) // lq
        ch = lax.broadcasted_iota(jnp.int32, (nrow, d), 1) // HEAD_DIM
        qbd_ref[...] = jnp.where(rh == ch, qt, 0.0).astype(BF16)
        m_ref[...] = jnp.full_like(m_ref, -jnp.inf)
        l_ref[...] = jnp.zeros_like(l_ref)
        acc_ref[...] = jnp.zeros_like(acc_ref)

    for cp in copies(g, slot):
        cp.wait()

    def update(s, values):
        m_prev = m_ref[...]
        m_new = jnp.maximum(m_prev, jnp.max(s, axis=1, keepdims=True))
        alpha = jnp.exp(m_prev - m_new)
        p = jnp.exp(s - m_new)
        l_ref[...] = alpha * l_ref[...] + jnp.sum(p, axis=1, keepdims=True)
        m_ref[...] = m_new
        pv = jnp.dot(p.astype(BF16), values, preferred_element_type=F32)
        for h in range(N_HEADS):
            rows = slice(2 * lq * h, 2 * lq * (h + 1))
            acc_ref[h] = alpha[rows, :] * acc_ref[h] + pv[rows, h * PAIR:(h + 1) * PAIR]

    kt = jnp.concatenate([kbuf[slot, p] for p in range(pages)], axis=1).astype(BF16)
    s = jnp.dot(qbd_ref[...], kt, preferred_element_type=F32)
    values = jnp.concatenate([vbuf[slot, pl.ds(h, tc, stride=N_HEADS), :].astype(BF16) for h in range(N_HEADS)],
                             axis=1)
    update(s, values)

    @pl.when(c_id == nc - 1)
    def _():
        pad = jnp.zeros((LANES - lq, d), F32)
        kn = jnp.concatenate([knew_ref[seq], pad], axis=0).astype(BF16)
        vn = jnp.concatenate([vnew_ref[seq], pad], axis=0).astype(BF16)
        s2 = lax.dot_general(qbd_ref[...], kn, (((1,), (1,)), ((), ())), preferred_element_type=F32)
        qi = lax.rem(lax.broadcasted_iota(jnp.int32, (nrow, LANES), 0), lq)
        kj = lax.broadcasted_iota(jnp.int32, (nrow, LANES), 1)
        s2 = jnp.where(kj <= qi, s2, -jnp.inf)
        update(s2, vn)
        inv_l = 1.0 / l_ref[...]
        outs = []
        for h in range(N_HEADS):
            rows = slice(2 * lq * h, 2 * lq * (h + 1))
            o = acc_ref[h] * inv_l[rows, :]
            out = o[:lq] - lam_ref[0, 0] * o[lq:]
            ms = jnp.mean(out * out, axis=1, keepdims=True)
            outs.append(out * lax.rsqrt(ms + NORM_EPS) * subln_ref[...] * (1.0 - LAM_INIT))
        out_ref[seq] = jnp.concatenate(outs, axis=1) * sza_ref[seq]


def _finish_kernel(a_ref, sga_ref, mp_ref, x_ref, wab_ref, wo_ref, g_ref, b_ref, y_ref):
    br_a = jnp.dot(a_ref[...].astype(BF16), wab_ref[...], preferred_element_type=F32)
    m = mp_ref[...] + sga_ref[...].astype(F32) * br_a
    out = jnp.dot(m.astype(BF16), wo_ref[...], preferred_element_type=F32)
    h = ALPHA * x_ref[...] + out
    mu = jnp.mean(h, axis=-1, keepdims=True)
    hc = h - mu
    var = jnp.mean(hc * hc, axis=-1, keepdims=True)
    y_ref[...] = hc * lax.rsqrt(var + NORM_EPS) * g_ref[...] + b_ref[...]


def _finish(a, sga, mp, x, wab, wo, ln_g, ln_b):
    rows, d = x.shape
    tm = min(FIN_TM, rows)
    row = lambda i: (i, 0)
    const = lambda i: (0, 0)
    return pl.pallas_call(
        _finish_kernel,
        grid=(rows // tm,),
        in_specs=[pl.BlockSpec((tm, d), row), pl.BlockSpec((tm, d), row), pl.BlockSpec((tm, d), row),
                  pl.BlockSpec((tm, d), row), pl.BlockSpec((d, d), const), pl.BlockSpec((d, d), const),
                  pl.BlockSpec((1, d), const), pl.BlockSpec((1, d), const)],
        out_specs=pl.BlockSpec((tm, d), row),
        out_shape=jax.ShapeDtypeStruct((rows, d), F32),
        compiler_params=pltpu.CompilerParams(dimension_semantics=("arbitrary",), vmem_limit_bytes=VMEM_LIMIT),
        name="finish",
    )(a, sga, mp, x, wab, wo, ln_g, ln_b)


def kernel(x_prompt, x_sample, cache_k, cache_v, state_pool, page_table, w_in, w_pool_mix, pool_scale, lambda_q1, lambda_k1, lambda_q2, lambda_k2, subln_w, w_pool_branch, w_attn_branch, w_o, ln_g, ln_b):
    assert w_in.shape[0] == DEPTH
    nb, L, d = x_prompt.shape
    ns, ls, _ = x_sample.shape
    n_phys, page_size = cache_k.shape[1], cache_k.shape[2]
    assert page_size == LANES
    past_len = page_table.shape[1] * page_size

    lam = (jnp.exp(jnp.sum(lambda_q1[0] * lambda_k1[0])) - jnp.exp(jnp.sum(lambda_q2[0] * lambda_k2[0]))
           + LAM_INIT).reshape(1, 1).astype(F32)
    w_in_b = w_in[0].astype(BF16)
    wmix_b = w_pool_mix[0].astype(BF16)
    wpb_b = w_pool_branch[0].astype(BF16)
    wab_b = w_attn_branch[0].astype(BF16)
    wo_b = w_o[0].astype(BF16)
    pscale = pool_scale[0].reshape(1, POOL_WIDTH)
    g_row = ln_g[0].reshape(1, d)
    b_row = ln_b[0].reshape(1, d)

    tabs_p = _rope_tables(jnp.arange(L))
    k_p, v_p, qp, kp, vt, sza, sga, mp, tail = _proj_prompt(x_prompt, tabs_p, w_in_b, wmix_b, pscale, wpb_b)
    pos_s = jnp.tile(past_len + jnp.arange(ls), ns)
    tabs_s = _rope_tables(pos_s)
    prev16 = jnp.pad(state_pool[0], ((0, 0), (16 - POOL_CTX, 0), (0, 0)))
    x_s2 = x_sample.reshape(ns * ls, d)
    k_s, v_s, q_s, sza_s, sga_s, mp_s, ext_s = _proj_sample(x_s2, tabs_s, prev16, w_in_b, wmix_b, pscale, wpb_b,
                                                            ns, ls, past_len)
    page_rows = page_size * d // LANES
    ck2 = jnp.transpose(cache_k[0], (0, 2, 3, 1)).reshape(n_phys * page_rows, LANES)
    cv2 = cache_v[0].reshape(n_phys * page_rows, LANES)
    r3 = lambda z: z.reshape(ns, ls, d)
    a_p, a_s = _attn_fused(page_table, lam, qp, kp, vt, sza, subln_w[0].reshape(PAIR, 1),
                           r3(q_s), r3(k_s), r3(v_s), r3(sza_s), subln_w[0].reshape(1, PAIR), ck2, cv2, page_rows)
    y_p = _finish(a_p.reshape(nb * L, d), sga.reshape(nb * L, d), mp.reshape(nb * L, d),
                  x_prompt.reshape(nb * L, d), wab_b, wo_b, g_row, b_row).reshape(nb, L, d)
    y_s = _finish(a_s.reshape(ns * ls, d), sga_s, mp_s, x_s2, wab_b, wo_b, g_row, b_row).reshape(ns, ls, d)

    return (y_p, y_s,
            k_p.reshape(1, nb, L, 2 * N_HEADS, HEAD_DIM), v_p.reshape(1, nb, L, N_HEADS, PAIR),
            tail[:, 16 - POOL_CTX:][None],
            k_s.reshape(1, ns, ls, 2 * N_HEADS, HEAD_DIM), v_s.reshape(1, ns, ls, N_HEADS, PAIR),
            ext_s[:, 16 + ls - POOL_CTX:][None])
```

```python
import functools
import math

import jax
import jax.numpy as jnp
from jax import lax
from jax.experimental import pallas as pl
from jax.experimental.pallas import tpu as pltpu

F32 = jnp.float32
BF16 = jnp.bfloat16

N_HEADS = 8
HEAD_DIM = 64
PAIR = 2 * HEAD_DIM
VT_ROWS = PAIR + 16
POOL_WINDOWS = (2, 4, 8, 16)
POOL_GC = 128
POOL_WIDTH = POOL_GC * len(POOL_WINDOWS)
POOL_CTX = max(POOL_WINDOWS) - 1
ROPE_THETA = 10000.0
NORM_EPS = 1e-5
DEPTH = 1
ALPHA = (2.0 * DEPTH) ** 0.25
LAM_INIT = 0.8 - 0.6 * math.exp(-0.3 * 0)
LOG2E = math.log2(math.e)

C_UP, C_ZP, C_Q, C_K, C_V, C_ZA, C_GP, C_GA, C_END = (0, 512, 1024, 2048, 3072, 4096, 5120, 6144, 7168)

LANES = 128
POOL_HIST = 32
VMEM_LIMIT = 56 * 1024 * 1024

PROJ_TM = 256
ATTN_T = 512
ATTN_HPS = 4
FIN_TM = 512
PAGES_PER_CHUNK = 8


def _silu(z):
    return z * jax.nn.sigmoid(z)


def _rope_block(x, c, sn, sp):
    return x * c + pltpu.roll(x, LANES - 32, axis=1) * sn + pltpu.roll(x, 32, axis=1) * sp


def _rope_tables(pos):
    inv_freq = 1.0 / (ROPE_THETA ** (jnp.arange(0, HEAD_DIM, 2, dtype=F32) / HEAD_DIM))
    ang = pos.astype(F32)[:, None] * inv_freq[None, :]
    cos = jnp.cos(ang)
    sin = jnp.sin(ang)
    z = jnp.zeros_like(sin)
    c = jnp.tile(cos, (1, 4))
    sn = jnp.tile(jnp.concatenate([-sin, z], axis=1), (1, 2))
    sp = jnp.tile(jnp.concatenate([z, sin], axis=1), (1, 2))
    return c, sn, sp


def _pool_mix(sums, u, pos, wmix_ref, pscale_ref):
    outs = []
    for g, w in enumerate(POOL_WINDOWS):
        cols = slice(g * POOL_GC, (g + 1) * POOL_GC)
        cnt = jnp.minimum(pos + 1, w).astype(F32)
        d = sums[g] / cnt - u[:, cols]
        mixed = jnp.dot(d.astype(BF16), wmix_ref[g], preferred_element_type=F32)
        outs.append(mixed * pscale_ref[:, cols])
    return jnp.concatenate(outs, axis=1)


def _proj_prompt_kernel(x_ref, c_ref, sn_ref, sp_ref, w_ref, wmix_ref, pscale_ref, wpb_ref,
                        k_ref, v_ref, qp_ref, kp_ref, vt_ref, sza_ref, sga_ref, mp_ref, tail_ref,
                        ext_ref, sb_ref, *, tm):
    t = pl.program_id(1)
    n = POOL_HIST + tm
    xb = x_ref[0].astype(BF16)

    def proj(lo, hi):
        return jnp.dot(xb, w_ref[:, lo:hi], preferred_element_type=F32)

    u = proj(C_UP, C_ZP)

    @pl.when(t == 0)
    def _():
        ext_ref[0:POOL_HIST, :] = jnp.zeros((POOL_HIST, POOL_WIDTH), F32)

    ext_ref[POOL_HIST:n, :] = u
    sb_ref[0, 8:n, :] = ext_ref[8:n, :] + ext_ref[7:n - 1, :]
    sb_ref[1, 16:n, 128:] = sb_ref[0, 16:n, 128:] + sb_ref[0, 14:n - 2, 128:]
    sb_ref[2, 24:n, 256:] = sb_ref[1, 24:n, 256:] + sb_ref[1, 20:n - 4, 256:]
    s16 = sb_ref[2, 32:n, 384:] + sb_ref[2, 24:n - 8, 384:]
    sums = [sb_ref[0, 32:n, 0:128], sb_ref[1, 32:n, 128:256], sb_ref[2, 32:n, 256:384], s16]
    tail_ref[0] = ext_ref[n - 16:n, :]
    ext_ref[0:POOL_HIST, :] = ext_ref[tm:n, :]

    pos = t * tm + lax.broadcasted_iota(jnp.int32, (tm, POOL_GC), 0)
    pool_o = _pool_mix(sums, u, pos, wmix_ref, pscale_ref)
    zp = proj(C_ZP, C_Q)
    br_p = jnp.dot((pool_o * _silu(zp)).astype(BF16), wpb_ref[...], preferred_element_type=F32)
    mp_ref[0] = jax.nn.sigmoid(proj(C_GP, C_GA)) * br_p

    c = c_ref[...]
    sn = sn_ref[...]
    sp = sp_ref[...]
    q = proj(C_Q, C_K) * (HEAD_DIM ** -0.5 * LOG2E)
    for h in range(N_HEADS):
        cols = slice(h * PAIR, (h + 1) * PAIR)
        qp_ref[0, h] = _rope_block(q[:, cols], c, sn, sp).astype(BF16)
    k = proj(C_K, C_V)
    for h in range(N_HEADS):
        cols = slice(h * PAIR, (h + 1) * PAIR)
        kr = _rope_block(k[:, cols], c, sn, sp)
        k_ref[0, :, cols] = kr
        kp_ref[0, h] = kr.astype(BF16)
    v = proj(C_V, C_ZA)
    v_ref[0] = v
    for h in range(N_HEADS):
        vt_ref[0, h, 0, 0:PAIR, :] = v[:, h * PAIR:(h + 1) * PAIR].T.astype(BF16)
        vt_ref[0, h, 0, PAIR:VT_ROWS, :] = jnp.ones((VT_ROWS - PAIR, tm), BF16)
    sza_ref[0] = _silu(proj(C_ZA, C_GP)).astype(BF16)
    sga_ref[0] = jax.nn.sigmoid(proj(C_GA, C_END)).astype(BF16)


def _proj_prompt(x, tabs, w_in, wmix, pscale, wpb):
    nb, L, d = x.shape
    tm = PROJ_TM
    nt = L // tm
    const = dict(pipeline_mode=pl.Buffered(1))
    row = lambda b, t: (b, t, 0)
    head = lambda b, t: (b, 0, t, 0)
    in_specs = [
        pl.BlockSpec((1, tm, d), row),
        pl.BlockSpec((tm, LANES), lambda b, t: (t, 0)),
        pl.BlockSpec((tm, LANES), lambda b, t: (t, 0)),
        pl.BlockSpec((tm, LANES), lambda b, t: (t, 0)),
        pl.BlockSpec((d, C_END), lambda b, t: (0, 0), **const),
        pl.BlockSpec((len(POOL_WINDOWS), POOL_GC, POOL_GC), lambda b, t: (0, 0, 0), **const),
        pl.BlockSpec((1, POOL_WIDTH), lambda b, t: (0, 0), **const),
        pl.BlockSpec((POOL_WIDTH, d), lambda b, t: (0, 0), **const),
    ]
    out_specs = [
        pl.BlockSpec((1, tm, d), row),
        pl.BlockSpec((1, tm, d), row),
        pl.BlockSpec((1, N_HEADS, tm, PAIR), head),
        pl.BlockSpec((1, N_HEADS, tm, PAIR), head),
        pl.BlockSpec((1, N_HEADS, 1, VT_ROWS, tm), lambda b, t: (b, 0, t, 0, 0)),
        pl.BlockSpec((1, tm, d), row),
        pl.BlockSpec((1, tm, d), row),
        pl.BlockSpec((1, tm, d), row),
        pl.BlockSpec((1, 16, POOL_WIDTH), lambda b, t: (b, 0, 0)),
    ]
    out_shape = [
        jax.ShapeDtypeStruct((nb, L, d), F32),
        jax.ShapeDtypeStruct((nb, L, d), F32),
        jax.ShapeDtypeStruct((nb, N_HEADS, L, PAIR), BF16),
        jax.ShapeDtypeStruct((nb, N_HEADS, L, PAIR), BF16),
        jax.ShapeDtypeStruct((nb, N_HEADS, nt, VT_ROWS, tm), BF16),
        jax.ShapeDtypeStruct((nb, L, d), BF16),
        jax.ShapeDtypeStruct((nb, L, d), BF16),
        jax.ShapeDtypeStruct((nb, L, d), F32),
        jax.ShapeDtypeStruct((nb, 16, POOL_WIDTH), F32),
    ]
    return pl.pallas_call(
        functools.partial(_proj_prompt_kernel, tm=tm),
        grid=(nb, nt),
        in_specs=in_specs,
        out_specs=out_specs,
        out_shape=out_shape,
        scratch_shapes=[pltpu.VMEM((POOL_HIST + tm, POOL_WIDTH), F32),
                        pltpu.VMEM((3, POOL_HIST + tm, POOL_WIDTH), F32)],
        compiler_params=pltpu.CompilerParams(dimension_semantics=("arbitrary", "arbitrary"),
                                             vmem_limit_bytes=VMEM_LIMIT),
        name="proj_prompt",
    )(x, *tabs, w_in, wmix, pscale, wpb)


def _proj_sample_kernel(x_ref, c_ref, sn_ref, sp_ref, prev_ref, w_ref, wmix_ref, pscale_ref, wpb_ref,
                        k_ref, v_ref, q_ref, sza_ref, sga_ref, mp_ref, ext_ref, *, ns, ls, pos0):
    rows = ns * ls
    xb = x_ref[...].astype(BF16)

    def proj(lo, hi):
        return jnp.dot(xb, w_ref[:, lo:hi], preferred_element_type=F32)

    u = proj(C_UP, C_ZP)
    ext_ref[:, 0:16, :] = prev_ref[...]
    ext_ref[:, 16:16 + ls, :] = u.reshape(ns, ls, POOL_WIDTH)
    sums = []
    for g, w in enumerate(POOL_WINDOWS):
        cols = slice(g * POOL_GC, (g + 1) * POOL_GC)
        s = ext_ref[:, 16:16 + ls, cols]
        for j in range(1, w):
            s = s + ext_ref[:, 16 - j:16 + ls - j, cols]
        sums.append(s.reshape(rows, POOL_GC))
    pos = pos0 + lax.broadcasted_iota(jnp.int32, (ns, ls, POOL_GC), 1).reshape(rows, POOL_GC)
    pool_o = _pool_mix(sums, u, pos, wmix_ref, pscale_ref)
    zp = proj(C_ZP, C_Q)
    br_p = jnp.dot((pool_o * _silu(zp)).astype(BF16), wpb_ref[...], preferred_element_type=F32)
    mp_ref[...] = jax.nn.sigmoid(proj(C_GP, C_GA)) * br_p

    c = c_ref[...]
    sn = sn_ref[...]
    sp = sp_ref[...]
    q = proj(C_Q, C_K) * (HEAD_DIM ** -0.5)
    k = proj(C_K, C_V)
    for h in range(N_HEADS):
        cols = slice(h * PAIR, (h + 1) * PAIR)
        q_ref[:, cols] = _rope_block(q[:, cols], c, sn, sp)
        k_ref[:, cols] = _rope_block(k[:, cols], c, sn, sp)
    v_ref[...] = proj(C_V, C_ZA)
    sza_ref[...] = _silu(proj(C_ZA, C_GP))
    sga_ref[...] = jax.nn.sigmoid(proj(C_GA, C_END))


def _proj_sample(x2, tabs, prev16, w_in, wmix, pscale, wpb, ns, ls, pos0):
    rows, d = x2.shape
    big = jax.ShapeDtypeStruct((rows, d), F32)
    return pl.pallas_call(
        functools.partial(_proj_sample_kernel, ns=ns, ls=ls, pos0=pos0),
        out_shape=[big, big, big, big, big, big, jax.ShapeDtypeStruct((ns, 16 + ls, POOL_WIDTH), F32)],
        compiler_params=pltpu.CompilerParams(vmem_limit_bytes=VMEM_LIMIT),
        name="proj_sample",
    )(x2, *tabs, prev16, w_in, wmix, pscale, wpb)


COL_REDUCE_ROWS = 64


def _col_reduce(x, op):
    rows, cols = x.shape
    slabs = op(x.reshape(rows // COL_REDUCE_ROWS, COL_REDUCE_ROWS, cols), axis=0)
    return op(slabs, axis=0, keepdims=True)


def _attn_units(kts, vts, ws, ms, ls, acc_ref, mask, between=None):
    def scores(u):
        return lax.dot_general(kts[u], ws[u], (((1,), (1,)), ((), ())), preferred_element_type=F32)

    n = len(ws)
    ms_new, ls_new = [], []
    s_next = scores(0)
    if between is not None:
        between()
    for u in range(n):
        s = s_next
        if u + 1 < n:
            s_next = scores(u + 1)
        if mask is not None:
            s = jnp.where(mask, s, -jnp.inf)
        m_new = jnp.maximum(ms[u], _col_reduce(s, jnp.max))
        alpha = jnp.exp2(ms[u] - m_new)
        p = jnp.exp2(s - m_new)
        ms_new.append(m_new)
        pv = jnp.dot(vts[u], p.astype(BF16), preferred_element_type=F32)
        acc_ref[u] = alpha * acc_ref[u] + pv[:PAIR]
        ls_new.append(alpha * ls[u] + pv[PAIR:PAIR + 1])
    return tuple(ms_new), tuple(ls_new)


def _attn_fused_kernel(pt_ref, lam_ref, q_ref, k_ref, vt_ref, sza_ref, subln_ref,
                       qs_ref, knew_ref, vnew_ref, szas_ref, sublnr_ref, ck_hbm, cv_hbm,
                       out_ref, dout_ref,
                       acc_ref, kbuf, vbuf, sem, qbd_ref, dm_ref, dl_ref, dacc_ref, gcount_ref,
                       *, t, cv, hps, pages, nc, page_rows, total):
    j = pl.program_id(2)
    lin = (pl.program_id(0) * pl.num_programs(1) + pl.program_id(1)) * pl.num_programs(2) + j
    n_lin = pl.num_programs(0) * pl.num_programs(1) * pl.num_programs(2)
    dma_refs = (pt_ref, ck_hbm, cv_hbm, kbuf, vbuf, sem)
    dec_kw = dict(pages=pages, nc=nc, page_rows=page_rows)

    @pl.when(lin == 0)
    def _():
        gcount_ref[0] = 0
        for cp in _decode_copies(dma_refs, 0, 0, **dec_kw):
            cp.start()

    g0 = gcount_ref[0]

    def decode_parts(g):
        return _decode_parts(g, dma_refs, lam_ref, qs_ref, knew_ref, vnew_ref, szas_ref, sublnr_ref, dout_ref,
                             qbd_ref, dm_ref, dl_ref, dacc_ref, total=total, **dec_kw)

    lane = lax.broadcasted_iota(jnp.int32, (t, PAIR), 1)
    ws = []
    for u in range(hps):
        q = q_ref[0, u]
        zero = jnp.zeros_like(q)
        ws.append(jnp.concatenate([jnp.where(lane < HEAD_DIM, q, zero), jnp.where(lane >= HEAD_DIM, q, zero)],
                                  axis=0))
    acc_ref[...] = jnp.zeros_like(acc_ref)
    nck = t // cv

    def tile(i, ms, ls, mask):
        pre, dec_scores, dec_update, post = decode_parts(g0 + i)
        pre()
        kts = [k_ref[0, u, pl.ds(pl.multiple_of(i * t, t), t), :] for u in range(hps)]
        vts = [jnp.concatenate([vt_ref[0, u, i * nck + c] for c in range(nck)], axis=1) for u in range(hps)]
        s_dec = dec_scores()
        ms, ls = _attn_units(kts, vts, ws, ms, ls, acc_ref, mask, between=lambda: dec_update(s_dec))
        post()
        return ms, ls

    m0 = tuple(jnp.full((1, 2 * t), -jnp.inf, F32) for _ in range(hps))
    l0 = tuple(jnp.zeros((1, 2 * t), F32) for _ in range(hps))
    ms, ls = lax.fori_loop(0, j, lambda i, c: tile(i, c[0], c[1], None), (m0, l0))
    krow = lax.broadcasted_iota(jnp.int32, (t, 2 * t), 0)
    qcol = lax.broadcasted_iota(jnp.int32, (t, 2 * t), 1)
    qcol = jnp.where(qcol >= t, qcol - t, qcol)
    ms, ls = tile(j, ms, ls, krow <= qcol)

    for u in range(hps):
        o = acc_ref[u] * (1.0 / ls[u])
        out = o[:, :t] - lam_ref[0, 0] * o[:, t:]
        msq = jnp.mean(out * out, axis=0, keepdims=True)
        out = out * lax.rsqrt(msq + NORM_EPS) * subln_ref[...] * (1.0 - LAM_INIT)
        cols = slice(u * PAIR, (u + 1) * PAIR)
        out_ref[0, :, cols] = (out.T * sza_ref[0, :, cols].astype(F32)).astype(BF16)

    g_next = g0 + j + 1
    gcount_ref[0] = g_next

    @pl.when(lin == n_lin - 1)
    def _():
        def body(g, carry):
            pre, dec_scores, dec_update, post = decode_parts(g)
            pre()
            dec_update(dec_scores())
            post()
            return carry
        lax.fori_loop(g_next, total, body, 0)


def _attn_fused(page_table, lam, qp, kp, vt, sza, subln_col, q3, knew3, vnew3, sza3, subln_row, ck2, cv2,
                page_rows):
    nb, nh, L, _ = qp.shape
    ns, lq, d = q3.shape
    t = ATTN_T
    hps = ATTN_HPS
    nchunk, cv = vt.shape[2], vt.shape[4]
    n_pages = page_table.shape[1]
    pages = PAGES_PER_CHUNK
    assert n_pages % pages == 0
    nc = n_pages // pages
    nrow = 2 * N_HEADS * lq
    once = dict(pipeline_mode=pl.Buffered(1))
    whole = lambda b, h, j, pt: (0, 0, 0)
    grid_spec = pltpu.PrefetchScalarGridSpec(
        num_scalar_prefetch=1,
        grid=(nb, nh // hps, L // t),
        in_specs=[
            pl.BlockSpec(memory_space=pltpu.SMEM),
            pl.BlockSpec((1, hps, t, PAIR), lambda b, h, j, pt: (b, h, j, 0)),
            pl.BlockSpec((1, hps, L, PAIR), lambda b, h, j, pt: (b, h, 0, 0), **once),
            pl.BlockSpec((1, hps, nchunk, VT_ROWS, cv), lambda b, h, j, pt: (b, h, 0, 0, 0), **once),
            pl.BlockSpec((1, t, hps * PAIR), lambda b, h, j, pt: (b, j, h)),
            pl.BlockSpec((PAIR, 1), lambda b, h, j, pt: (0, 0)),
            pl.BlockSpec((ns, lq, d), whole, **once),
            pl.BlockSpec((ns, lq, d), whole, **once),
            pl.BlockSpec((ns, lq, d), whole, **once),
            pl.BlockSpec((ns, lq, d), whole, **once),
            pl.BlockSpec((1, PAIR), lambda b, h, j, pt: (0, 0)),
            pl.BlockSpec(memory_space=pl.ANY),
            pl.BlockSpec(memory_space=pl.ANY),
        ],
        out_specs=[
            pl.BlockSpec((1, t, hps * PAIR), lambda b, h, j, pt: (b, j, h)),
            pl.BlockSpec((ns, lq, d), whole),
        ],
        scratch_shapes=[
            pltpu.VMEM((hps, PAIR, 2 * t), F32),
            pltpu.VMEM((2, pages, page_rows, LANES), F32),
            pltpu.VMEM((2, pages * page_rows, LANES), F32),
            pltpu.SemaphoreType.DMA((2, 2)),
            pltpu.VMEM((nrow, d), BF16),
            pltpu.VMEM((nrow, 1), F32),
            pltpu.VMEM((nrow, 1), F32),
            pltpu.VMEM((N_HEADS, 2 * lq, PAIR), F32),
            pltpu.SMEM((1,), jnp.int32),
        ],
    )
    return pl.pallas_call(
        functools.partial(_attn_fused_kernel, t=t, cv=cv, hps=hps, pages=pages, nc=nc, page_rows=page_rows,
                          total=ns * nc),
        grid_spec=grid_spec,
        out_shape=[jax.ShapeDtypeStruct((nb, L, nh * PAIR), BF16), jax.ShapeDtypeStruct((ns, lq, d), F32)],
        compiler_params=pltpu.CompilerParams(dimension_semantics=("arbitrary", "arbitrary", "arbitrary"),
                                             vmem_limit_bytes=VMEM_LIMIT),
        name="attn_fused",
    )(page_table, lam, qp, kp, vt, sza, subln_col, q3, knew3, vnew3, sza3, subln_row, ck2, cv2)


def _divmod(g, n):
    if isinstance(g, int):
        return divmod(g, n)
    return lax.div(g, jnp.int32(n)), lax.rem(g, jnp.int32(n))


def _decode_copies(refs, g, slot, *, pages, nc, page_rows):
    pt_ref, ck_hbm, cv_hbm, kbuf, vbuf, sem = refs
    seq, chunk = _divmod(g, nc)
    cps = []
    for p in range(pages):
        row0 = pl.multiple_of(pt_ref[seq, chunk * pages + p] * page_rows, page_rows)
        cps.append(pltpu.make_async_copy(ck_hbm.at[pl.ds(row0, page_rows), :], kbuf.at[slot, p], sem.at[0, slot]))
        cps.append(pltpu.make_async_copy(cv_hbm.at[pl.ds(row0, page_rows), :],
                                         vbuf.at[slot, pl.ds(p * page_rows, page_rows), :], sem.at[1, slot]))
    return cps


def _decode_parts(g, dma_refs, lam_ref, q_ref, knew_ref, vnew_ref, sza_ref, subln_ref, out_ref,
                  qbd_ref, m_ref, l_ref, acc_ref, *, pages, nc, page_rows, total):
    kbuf, vbuf = dma_refs[3], dma_refs[4]
    lq, d = q_ref.shape[1], q_ref.shape[2]
    nrow = 2 * N_HEADS * lq
    tc = pages * LANES
    active = g < total
    gc = jnp.minimum(g, total - 1)
    seq, c_id = _divmod(gc, nc)
    slot = lax.rem(gc, 2)
    copies = functools.partial(_decode_copies, dma_refs, pages=pages, nc=nc, page_rows=page_rows)

    def pre():
        @pl.when(g + 1 < total)
        def _():
            for cp in copies(g + 1, 1 - slot):
                cp.start()

        @pl.when(active & (c_id == 0))
        def _():
            qt = jnp.concatenate([q_ref[seq]] * (2 * N_HEADS), axis=0)
            rh = lax.broadcasted_iota(jnp.int32, (nrow, d), 0) // lq
            ch = lax.broadcasted_iota(jnp.int32, (nrow, d), 1) // HEAD_DIM
            qbd_ref[...] = jnp.where(rh == ch, qt, 0.0).astype(BF16)
            m_ref[...] = jnp.full_like(m_ref, -jnp.inf)
            l_ref[...] = jnp.zeros_like(l_ref)
            acc_ref[...] = jnp.zeros_like(acc_ref)

        @pl.when(active)
        def _():
            for cp in copies(g, slot):
                cp.wait()

    def update(s, values):
        m_prev = m_ref[...]
        m_new = jnp.maximum(m_prev, jnp.max(s, axis=1, keepdims=True))
        alpha = jnp.exp(m_prev - m_new)
        p = jnp.exp(s - m_new)
        l_ref[...] = alpha * l_ref[...] + jnp.sum(p, axis=1, keepdims=True)
        m_ref[...] = m_new
        pv = jnp.dot(p.astype(BF16), values, preferred_element_type=F32)
        for h in range(N_HEADS):
            rows = slice(2 * lq * h, 2 * lq * (h + 1))
            acc_ref[h] = alpha[rows, :] * acc_ref[h] + pv[rows, h * PAIR:(h + 1) * PAIR]

    def scores():
        kt = jnp.concatenate([kbuf[slot, p] for p in range(pages)], axis=1).astype(BF16)
        return jnp.dot(qbd_ref[...], kt, preferred_element_type=F32)

    def update_cached(s):
        values = jnp.concatenate(
            [vbuf[slot, pl.ds(h, tc, stride=N_HEADS), :].astype(BF16) for h in range(N_HEADS)], axis=1)
        update(s, values)

    def post():
        pl.when(active & (c_id == nc - 1))(finish)

    def finish():
        pad = jnp.zeros((LANES - lq, d), F32)
        kn = jnp.concatenate([knew_ref[seq], pad], axis=0).astype(BF16)
        vn = jnp.concatenate([vnew_ref[seq], pad], axis=0).astype(BF16)
        s2 = lax.dot_general(qbd_ref[...], kn, (((1,), (1,)), ((), ())), preferred_element_type=F32)
        qi = lax.rem(lax.broadcasted_iota(jnp.int32, (nrow, LANES), 0), lq)
        kj = lax.broadcasted_iota(jnp.int32, (nrow, LANES), 1)
        s2 = jnp.where(kj <= qi, s2, -jnp.inf)
        update(s2, vn)
        inv_l = 1.0 / l_ref[...]
        outs = []
        for h in range(N_HEADS):
            rows = slice(2 * lq * h, 2 * lq * (h + 1))
            o = acc_ref[h] * inv_l[rows, :]
            out = o[:lq] - lam_ref[0, 0] * o[lq:]
            ms = jnp.mean(out * out, axis=1, keepdims=True)
            outs.append(out * lax.rsqrt(ms + NORM_EPS) * subln_ref[...] * (1.0 - LAM_INIT))
        out_ref[seq] = jnp.concatenate(outs, axis=1) * sza_ref[seq]

    return pre, scores, update_cached, post


def _finish_kernel(a_ref, sga_ref, mp_ref, x_ref, wab_ref, wo_ref, g_ref, b_ref, y_ref):
    br_a = jnp.dot(a_ref[...].astype(BF16), wab_ref[...], preferred_element_type=F32)
    m = mp_ref[...] + sga_ref[...].astype(F32) * br_a
    out = jnp.dot(m.astype(BF16), wo_ref[...], preferred_element_type=F32)
    h = ALPHA * x_ref[...] + out
    mu = jnp.mean(h, axis=-1, keepdims=True)
    hc = h - mu
    var = jnp.mean(hc * hc, axis=-1, keepdims=True)
    y_ref[...] = hc * lax.rsqrt(var + NORM_EPS) * g_ref[...] + b_ref[...]


def _finish(a, sga, mp, x, wab, wo, ln_g, ln_b):
    rows, d = x.shape
    tm = min(FIN_TM, rows)
    row = lambda i: (i, 0)
    const = lambda i: (0, 0)
    return pl.pallas_call(
        _finish_kernel,
        grid=(rows // tm,),
        in_specs=[pl.BlockSpec((tm, d), row), pl.BlockSpec((tm, d), row), pl.BlockSpec((tm, d), row),
                  pl.BlockSpec((tm, d), row), pl.BlockSpec((d, d), const), pl.BlockSpec((d, d), const),
                  pl.BlockSpec((1, d), const), pl.BlockSpec((1, d), const)],
        out_specs=pl.BlockSpec((tm, d), row),
        out_shape=jax.ShapeDtypeStruct((rows, d), F32),
        compiler_params=pltpu.CompilerParams(dimension_semantics=("arbitrary",), vmem_limit_bytes=VMEM_LIMIT),
        name="finish",
    )(a, sga, mp, x, wab, wo, ln_g, ln_b)


def kernel(x_prompt, x_sample, cache_k, cache_v, state_pool, page_table, w_in, w_pool_mix, pool_scale, lambda_q1, lambda_k1, lambda_q2, lambda_k2, subln_w, w_pool_branch, w_attn_branch, w_o, ln_g, ln_b):
    assert w_in.shape[0] == DEPTH
    nb, L, d = x_prompt.shape
    ns, ls, _ = x_sample.shape
    n_phys, page_size = cache_k.shape[1], cache_k.shape[2]
    assert page_size == LANES
    past_len = page_table.shape[1] * page_size

    lam = (jnp.exp(jnp.sum(lambda_q1[0] * lambda_k1[0])) - jnp.exp(jnp.sum(lambda_q2[0] * lambda_k2[0]))
           + LAM_INIT).reshape(1, 1).astype(F32)
    w_in_b = w_in[0].astype(BF16)
    wmix_b = w_pool_mix[0].astype(BF16)
    wpb_b = w_pool_branch[0].astype(BF16)
    wab_b = w_attn_branch[0].astype(BF16)
    wo_b = w_o[0].astype(BF16)
    pscale = pool_scale[0].reshape(1, POOL_WIDTH)
    g_row = ln_g[0].reshape(1, d)
    b_row = ln_b[0].reshape(1, d)

    tabs_p = _rope_tables(jnp.arange(L))
    k_p, v_p, qp, kp, vt, sza, sga, mp, tail = _proj_prompt(x_prompt, tabs_p, w_in_b, wmix_b, pscale, wpb_b)
    pos_s = jnp.tile(past_len + jnp.arange(ls), ns)
    tabs_s = _rope_tables(pos_s)
    prev16 = jnp.pad(state_pool[0], ((0, 0), (16 - POOL_CTX, 0), (0, 0)))
    x_s2 = x_sample.reshape(ns * ls, d)
    k_s, v_s, q_s, sza_s, sga_s, mp_s, ext_s = _proj_sample(x_s2, tabs_s, prev16, w_in_b, wmix_b, pscale, wpb_b,
                                                            ns, ls, past_len)
    page_rows = page_size * d // LANES
    ck2 = jnp.transpose(cache_k[0], (0, 2, 3, 1)).reshape(n_phys * page_rows, LANES)
    cv2 = cache_v[0].reshape(n_phys * page_rows, LANES)
    r3 = lambda z: z.reshape(ns, ls, d)
    a_p, a_s = _attn_fused(page_table, lam, qp, kp, vt, sza, subln_w[0].reshape(PAIR, 1),
                           r3(q_s), r3(k_s), r3(v_s), r3(sza_s), subln_w[0].reshape(1, PAIR), ck2, cv2, page_rows)
    y_p = _finish(a_p.reshape(nb * L, d), sga.reshape(nb * L, d), mp.reshape(nb * L, d),
                  x_prompt.reshape(nb * L, d), wab_b, wo_b, g_row, b_row).reshape(nb, L, d)
    y_s = _finish(a_s.reshape(ns * ls, d), sga_s, mp_s, x_s2, wab_b, wo_b, g_row, b_row).reshape(ns, ls, d)

    return (y_p, y_s,
            k_p.reshape(1, nb, L, 2 * N_HEADS, HEAD_DIM), v_p.reshape(1, nb, L, N_HEADS, PAIR),
            tail[:, 16 - POOL_CTX:][None],
            k_s.reshape(1, ns, ls, 2 * N_HEADS, HEAD_DIM), v_s.reshape(1, ns, ls, N_HEADS, PAIR),
            ext_s[:, 16 + ls - POOL_CTX:][None])
```

```python
import functools
import math

import jax
import jax.numpy as jnp
from jax import lax
from jax.experimental import pallas as pl
from jax.experimental.pallas import tpu as pltpu

F32 = jnp.float32
BF16 = jnp.bfloat16

N_HEADS = 8
HEAD_DIM = 64
PAIR = 2 * HEAD_DIM
VT_ROWS = PAIR + 16
POOL_WINDOWS = (2, 4, 8, 16)
POOL_GC = 128
POOL_WIDTH = POOL_GC * len(POOL_WINDOWS)
POOL_CTX = max(POOL_WINDOWS) - 1
ROPE_THETA = 10000.0
NORM_EPS = 1e-5
DEPTH = 1
ALPHA = (2.0 * DEPTH) ** 0.25
LAM_INIT = 0.8 - 0.6 * math.exp(-0.3 * 0)
LOG2E = math.log2(math.e)

C_UP, C_ZP, C_Q, C_K, C_V, C_ZA, C_GP, C_GA, C_END = (0, 512, 1024, 2048, 3072, 4096, 5120, 6144, 7168)

LANES = 128
POOL_HIST = 32
VMEM_LIMIT = 56 * 1024 * 1024

PROJ_TM = 256
ATTN_T = 512
ATTN_HPS = 4
FIN_TM = 512
PAGES_PER_CHUNK = 8


def _silu(z):
    return z * jax.nn.sigmoid(z)


def _rope_block(x, c, sn, sp):
    return x * c + pltpu.roll(x, LANES - 32, axis=1) * sn + pltpu.roll(x, 32, axis=1) * sp


def _rope_tables(pos):
    inv_freq = 1.0 / (ROPE_THETA ** (jnp.arange(0, HEAD_DIM, 2, dtype=F32) / HEAD_DIM))
    ang = pos.astype(F32)[:, None] * inv_freq[None, :]
    cos = jnp.cos(ang)
    sin = jnp.sin(ang)
    z = jnp.zeros_like(sin)
    c = jnp.tile(cos, (1, 4))
    sn = jnp.tile(jnp.concatenate([-sin, z], axis=1), (1, 2))
    sp = jnp.tile(jnp.concatenate([z, sin], axis=1), (1, 2))
    return c, sn, sp


def _pool_mix(sums, u, pos, wmix_ref, pscale_ref):
    outs = []
    for g, w in enumerate(POOL_WINDOWS):
        cols = slice(g * POOL_GC, (g + 1) * POOL_GC)
        cnt = jnp.minimum(pos + 1, w).astype(F32)
        d = sums[g] / cnt - u[:, cols]
        mixed = jnp.dot(d.astype(BF16), wmix_ref[g], preferred_element_type=F32)
        outs.append(mixed * pscale_ref[:, cols])
    return jnp.concatenate(outs, axis=1)


def _proj_prompt_kernel(x_ref, c_ref, sn_ref, sp_ref, w_ref, wmix_ref, pscale_ref, wpb_ref,
                        k_ref, v_ref, qp_ref, kp_ref, vt_ref, sza_ref, sga_ref, mp_ref, tail_ref,
                        ext_ref, sb_ref, *, tm):
    t = pl.program_id(1)
    n = POOL_HIST + tm
    xb = x_ref[0].astype(BF16)

    def proj(lo, hi):
        return jnp.dot(xb, w_ref[:, lo:hi], preferred_element_type=F32)

    u = proj(C_UP, C_ZP)

    @pl.when(t == 0)
    def _():
        ext_ref[0:POOL_HIST, :] = jnp.zeros((POOL_HIST, POOL_WIDTH), F32)

    ext_ref[POOL_HIST:n, :] = u
    sb_ref[0, 8:n, :] = ext_ref[8:n, :] + ext_ref[7:n - 1, :]
    sb_ref[1, 16:n, 128:] = sb_ref[0, 16:n, 128:] + sb_ref[0, 14:n - 2, 128:]
    sb_ref[2, 24:n, 256:] = sb_ref[1, 24:n, 256:] + sb_ref[1, 20:n - 4, 256:]
    s16 = sb_ref[2, 32:n, 384:] + sb_ref[2, 24:n - 8, 384:]
    sums = [sb_ref[0, 32:n, 0:128], sb_ref[1, 32:n, 128:256], sb_ref[2, 32:n, 256:384], s16]
    tail_ref[0] = ext_ref[n - 16:n, :]
    ext_ref[0:POOL_HIST, :] = ext_ref[tm:n, :]

    pos = t * tm + lax.broadcasted_iota(jnp.int32, (tm, POOL_GC), 0)
    pool_o = _pool_mix(sums, u, pos, wmix_ref, pscale_ref)
    zp = proj(C_ZP, C_Q)
    br_p = jnp.dot((pool_o * _silu(zp)).astype(BF16), wpb_ref[...], preferred_element_type=F32)
    mp_ref[0] = jax.nn.sigmoid(proj(C_GP, C_GA)) * br_p

    c = c_ref[...]
    sn = sn_ref[...]
    sp = sp_ref[...]
    q = proj(C_Q, C_K) * (HEAD_DIM ** -0.5 * LOG2E)
    for h in range(N_HEADS):
        cols = slice(h * PAIR, (h + 1) * PAIR)
        qp_ref[0, h] = _rope_block(q[:, cols], c, sn, sp).astype(BF16)
    k = proj(C_K, C_V)
    for h in range(N_HEADS):
        cols = slice(h * PAIR, (h + 1) * PAIR)
        kr = _rope_block(k[:, cols], c, sn, sp)
        k_ref[0, :, cols] = kr
        kp_ref[0, h] = kr.astype(BF16)
    v = proj(C_V, C_ZA)
    v_ref[0] = v
    for h in range(N_HEADS):
        vt_ref[0, h, 0, 0:PAIR, :] = v[:, h * PAIR:(h + 1) * PAIR].T.astype(BF16)
        vt_ref[0, h, 0, PAIR:VT_ROWS, :] = jnp.ones((VT_ROWS - PAIR, tm), BF16)
    sza_ref[0] = _silu(proj(C_ZA, C_GP)).astype(BF16)
    sga_ref[0] = jax.nn.sigmoid(proj(C_GA, C_END)).astype(BF16)


def _proj_prompt(x, tabs, w_in, wmix, pscale, wpb):
    nb, L, d = x.shape
    tm = PROJ_TM
    nt = L // tm
    const = dict(pipeline_mode=pl.Buffered(1))
    row = lambda b, t: (b, t, 0)
    head = lambda b, t: (b, 0, t, 0)
    in_specs = [
        pl.BlockSpec((1, tm, d), row),
        pl.BlockSpec((tm, LANES), lambda b, t: (t, 0)),
        pl.BlockSpec((tm, LANES), lambda b, t: (t, 0)),
        pl.BlockSpec((tm, LANES), lambda b, t: (t, 0)),
        pl.BlockSpec((d, C_END), lambda b, t: (0, 0), **const),
        pl.BlockSpec((len(POOL_WINDOWS), POOL_GC, POOL_GC), lambda b, t: (0, 0, 0), **const),
        pl.BlockSpec((1, POOL_WIDTH), lambda b, t: (0, 0), **const),
        pl.BlockSpec((POOL_WIDTH, d), lambda b, t: (0, 0), **const),
    ]
    out_specs = [
        pl.BlockSpec((1, tm, d), row),
        pl.BlockSpec((1, tm, d), row),
        pl.BlockSpec((1, N_HEADS, tm, PAIR), head),
        pl.BlockSpec((1, N_HEADS, tm, PAIR), head),
        pl.BlockSpec((1, N_HEADS, 1, VT_ROWS, tm), lambda b, t: (b, 0, t, 0, 0)),
        pl.BlockSpec((1, tm, d), row),
        pl.BlockSpec((1, tm, d), row),
        pl.BlockSpec((1, tm, d), row),
        pl.BlockSpec((1, 16, POOL_WIDTH), lambda b, t: (b, 0, 0)),
    ]
    out_shape = [
        jax.ShapeDtypeStruct((nb, L, d), F32),
        jax.ShapeDtypeStruct((nb, L, d), F32),
        jax.ShapeDtypeStruct((nb, N_HEADS, L, PAIR), BF16),
        jax.ShapeDtypeStruct((nb, N_HEADS, L, PAIR), BF16),
        jax.ShapeDtypeStruct((nb, N_HEADS, nt, VT_ROWS, tm), BF16),
        jax.ShapeDtypeStruct((nb, L, d), BF16),
        jax.ShapeDtypeStruct((nb, L, d), BF16),
        jax.ShapeDtypeStruct((nb, L, d), F32),
        jax.ShapeDtypeStruct((nb, 16, POOL_WIDTH), F32),
    ]
    return pl.pallas_call(
        functools.partial(_proj_prompt_kernel, tm=tm),
        grid=(nb, nt),
        in_specs=in_specs,
        out_specs=out_specs,
        out_shape=out_shape,
        scratch_shapes=[pltpu.VMEM((POOL_HIST + tm, POOL_WIDTH), F32),
                        pltpu.VMEM((3, POOL_HIST + tm, POOL_WIDTH), F32)],
        compiler_params=pltpu.CompilerParams(dimension_semantics=("arbitrary", "arbitrary"),
                                             vmem_limit_bytes=VMEM_LIMIT),
        name="proj_prompt",
    )(x, *tabs, w_in, wmix, pscale, wpb)


def _proj_sample_kernel(x_ref, c_ref, sn_ref, sp_ref, prev_ref, w_ref, wmix_ref, pscale_ref, wpb_ref,
                        k_ref, v_ref, q_ref, sza_ref, sga_ref, mp_ref, ext_ref, *, ns, ls, pos0):
    rows = ns * ls
    xb = x_ref[...].astype(BF16)

    def proj(lo, hi):
        return jnp.dot(xb, w_ref[:, lo:hi], preferred_element_type=F32)

    u = proj(C_UP, C_ZP)
    ext_ref[:, 0:16, :] = prev_ref[...]
    ext_ref[:, 16:16 + ls, :] = u.reshape(ns, ls, POOL_WIDTH)
    sums = []
    for g, w in enumerate(POOL_WINDOWS):
        cols = slice(g * POOL_GC, (g + 1) * POOL_GC)
        s = ext_ref[:, 16:16 + ls, cols]
        for j in range(1, w):
            s = s + ext_ref[:, 16 - j:16 + ls - j, cols]
        sums.append(s.reshape(rows, POOL_GC))
    pos = pos0 + lax.broadcasted_iota(jnp.int32, (ns, ls, POOL_GC), 1).reshape(rows, POOL_GC)
    pool_o = _pool_mix(sums, u, pos, wmix_ref, pscale_ref)
    zp = proj(C_ZP, C_Q)
    br_p = jnp.dot((pool_o * _silu(zp)).astype(BF16), wpb_ref[...], preferred_element_type=F32)
    mp_ref[...] = jax.nn.sigmoid(proj(C_GP, C_GA)) * br_p

    c = c_ref[...]
    sn = sn_ref[...]
    sp = sp_ref[...]
    q = proj(C_Q, C_K) * (HEAD_DIM ** -0.5)
    k = proj(C_K, C_V)
    for h in range(N_HEADS):
        cols = slice(h * PAIR, (h + 1) * PAIR)
        q_ref[:, cols] = _rope_block(q[:, cols], c, sn, sp)
        k_ref[:, cols] = _rope_block(k[:, cols], c, sn, sp)
    v_ref[...] = proj(C_V, C_ZA)
    sza_ref[...] = _silu(proj(C_ZA, C_GP))
    sga_ref[...] = jax.nn.sigmoid(proj(C_GA, C_END))


def _proj_sample(x2, tabs, prev16, w_in, wmix, pscale, wpb, ns, ls, pos0):
    rows, d = x2.shape
    big = jax.ShapeDtypeStruct((rows, d), F32)
    return pl.pallas_call(
        functools.partial(_proj_sample_kernel, ns=ns, ls=ls, pos0=pos0),
        out_shape=[big, big, big, big, big, big, jax.ShapeDtypeStruct((ns, 16 + ls, POOL_WIDTH), F32)],
        compiler_params=pltpu.CompilerParams(vmem_limit_bytes=VMEM_LIMIT),
        name="proj_sample",
    )(x2, *tabs, prev16, w_in, wmix, pscale, wpb)


COL_REDUCE_ROWS = 64


def _col_reduce(x, op):
    rows, cols = x.shape
    slabs = op(x.reshape(rows // COL_REDUCE_ROWS, COL_REDUCE_ROWS, cols), axis=0)
    return op(slabs, axis=0, keepdims=True)


def _attn_units(kts, vts, ws, ms, ls, acc_ref, mask, between=None):
    def scores(u):
        s = lax.dot_general(kts[u], ws[u], (((1,), (1,)), ((), ())), preferred_element_type=F32)
        if between is not None and u < len(between):
            between[u]()
        return s

    n = len(ws)
    ms_new, ls_new = [], []
    s_next = scores(0)
    for u in range(n):
        s = s_next
        if u + 1 < n:
            s_next = scores(u + 1)
        if mask is not None:
            s = jnp.where(mask, s, -jnp.inf)
        m_new = jnp.maximum(ms[u], _col_reduce(s, jnp.max))
        alpha = jnp.exp2(ms[u] - m_new)
        p = jnp.exp2(s - m_new)
        ms_new.append(m_new)
        pv = jnp.dot(vts[u], p.astype(BF16), preferred_element_type=F32)
        acc_ref[u] = alpha * acc_ref[u] + pv[:PAIR]
        ls_new.append(alpha * ls[u] + pv[PAIR:PAIR + 1])
    return tuple(ms_new), tuple(ls_new)


def _attn_fused_kernel(pt_ref, lam_ref, q_ref, k_ref, vt_ref, sza_ref, subln_ref,
                       qs_ref, knew_ref, vnew_ref, szas_ref, sublnr_ref, ck_hbm, cv_hbm,
                       out_ref, dout_ref,
                       acc_ref, kbuf, vbuf, sem, qbd_ref, dm_ref, dl_ref, dacc_ref, gcount_ref,
                       *, t, cv, hps, pages, nc, page_rows, total):
    j = pl.program_id(2)
    lin = (pl.program_id(0) * pl.num_programs(1) + pl.program_id(1)) * pl.num_programs(2) + j
    n_lin = pl.num_programs(0) * pl.num_programs(1) * pl.num_programs(2)
    dma_refs = (pt_ref, ck_hbm, cv_hbm, kbuf, vbuf, sem)
    dec_kw = dict(pages=pages, nc=nc, page_rows=page_rows)

    @pl.when(lin == 0)
    def _():
        gcount_ref[0] = 0
        for cp in _decode_copies(dma_refs, 0, 0, **dec_kw):
            cp.start()

    g0 = gcount_ref[0]

    def decode_parts(g):
        return _decode_parts(g, dma_refs, lam_ref, qs_ref, knew_ref, vnew_ref, szas_ref, sublnr_ref, dout_ref,
                             qbd_ref, dm_ref, dl_ref, dacc_ref, total=total, **dec_kw)

    lane = lax.broadcasted_iota(jnp.int32, (t, PAIR), 1)
    ws = []
    for u in range(hps):
        q = q_ref[0, u]
        zero = jnp.zeros_like(q)
        ws.append(jnp.concatenate([jnp.where(lane < HEAD_DIM, q, zero), jnp.where(lane >= HEAD_DIM, q, zero)],
                                  axis=0))
    acc_ref[...] = jnp.zeros_like(acc_ref)
    nck = t // cv

    def tile(i, ms, ls, mask):
        pre, dec_scores, dec_update, post = decode_parts(g0 + i)
        pre()
        kts = [k_ref[0, u, pl.ds(pl.multiple_of(i * t, t), t), :] for u in range(hps)]
        vts = [jnp.concatenate([vt_ref[0, u, i * nck + c] for c in range(nck)], axis=1) for u in range(hps)]
        ms, ls = _attn_units(kts, vts, ws, ms, ls, acc_ref, mask, between=dec_update(dec_scores(), pieces=hps))
        post()
        return ms, ls

    m0 = tuple(jnp.full((1, 2 * t), -jnp.inf, F32) for _ in range(hps))
    l0 = tuple(jnp.zeros((1, 2 * t), F32) for _ in range(hps))
    ms, ls = lax.fori_loop(0, j, lambda i, c: tile(i, c[0], c[1], None), (m0, l0))
    krow = lax.broadcasted_iota(jnp.int32, (t, 2 * t), 0)
    qcol = lax.broadcasted_iota(jnp.int32, (t, 2 * t), 1)
    qcol = jnp.where(qcol >= t, qcol - t, qcol)
    ms, ls = tile(j, ms, ls, krow <= qcol)

    for u in range(hps):
        o = acc_ref[u] * (1.0 / ls[u])
        out = o[:, :t] - lam_ref[0, 0] * o[:, t:]
        msq = jnp.mean(out * out, axis=0, keepdims=True)
        out = out * lax.rsqrt(msq + NORM_EPS) * subln_ref[...] * (1.0 - LAM_INIT)
        cols = slice(u * PAIR, (u + 1) * PAIR)
        out_ref[0, :, cols] = (out.T * sza_ref[0, :, cols].astype(F32)).astype(BF16)

    g_next = g0 + j + 1
    gcount_ref[0] = g_next

    @pl.when(lin == n_lin - 1)
    def _():
        def body(g, carry):
            pre, dec_scores, dec_update, post = decode_parts(g)
            pre()
            for piece in dec_update(dec_scores()):
                piece()
            post()
            return carry
        lax.fori_loop(g_next, total, body, 0)


def _attn_fused(page_table, lam, qp, kp, vt, sza, subln_col, q3, knew3, vnew3, sza3, subln_row, ck2, cv2,
                page_rows):
    nb, nh, L, _ = qp.shape
    ns, lq, d = q3.shape
    t = ATTN_T
    hps = ATTN_HPS
    nchunk, cv = vt.shape[2], vt.shape[4]
    n_pages = page_table.shape[1]
    pages = PAGES_PER_CHUNK
    assert n_pages % pages == 0
    nc = n_pages // pages
    nrow = 2 * N_HEADS * lq
    once = dict(pipeline_mode=pl.Buffered(1))
    whole = lambda b, h, j, pt: (0, 0, 0)
    grid_spec = pltpu.PrefetchScalarGridSpec(
        num_scalar_prefetch=1,
        grid=(nb, nh // hps, L // t),
        in_specs=[
            pl.BlockSpec(memory_space=pltpu.SMEM),
            pl.BlockSpec((1, hps, t, PAIR), lambda b, h, j, pt: (b, h, j, 0)),
            pl.BlockSpec((1, hps, L, PAIR), lambda b, h, j, pt: (b, h, 0, 0), **once),
            pl.BlockSpec((1, hps, nchunk, VT_ROWS, cv), lambda b, h, j, pt: (b, h, 0, 0, 0), **once),
            pl.BlockSpec((1, t, hps * PAIR), lambda b, h, j, pt: (b, j, h)),
            pl.BlockSpec((PAIR, 1), lambda b, h, j, pt: (0, 0)),
            pl.BlockSpec((ns, lq, d), whole, **once),
            pl.BlockSpec((ns, lq, d), whole, **once),
            pl.BlockSpec((ns, lq, d), whole, **once),
            pl.BlockSpec((ns, lq, d), whole, **once),
            pl.BlockSpec((1, PAIR), lambda b, h, j, pt: (0, 0)),
            pl.BlockSpec(memory_space=pl.ANY),
            pl.BlockSpec(memory_space=pl.ANY),
        ],
        out_specs=[
            pl.BlockSpec((1, t, hps * PAIR), lambda b, h, j, pt: (b, j, h)),
            pl.BlockSpec((ns, lq, d), whole),
        ],
        scratch_shapes=[
            pltpu.VMEM((hps, PAIR, 2 * t), F32),
            pltpu.VMEM((2, pages, page_rows, LANES), F32),
            pltpu.VMEM((2, pages * page_rows, LANES), F32),
            pltpu.SemaphoreType.DMA((2, 2)),
            pltpu.VMEM((nrow, d), BF16),
            pltpu.VMEM((nrow, 1), F32),
            pltpu.VMEM((nrow, 1), F32),
            pltpu.VMEM((N_HEADS, 2 * lq, PAIR), F32),
            pltpu.SMEM((1,), jnp.int32),
        ],
    )
    return pl.pallas_call(
        functools.partial(_attn_fused_kernel, t=t, cv=cv, hps=hps, pages=pages, nc=nc, page_rows=page_rows,
                          total=ns * nc),
        grid_spec=grid_spec,
        out_shape=[jax.ShapeDtypeStruct((nb, L, nh * PAIR), BF16), jax.ShapeDtypeStruct((ns, lq, d), F32)],
        compiler_params=pltpu.CompilerParams(dimension_semantics=("arbitrary", "arbitrary", "arbitrary"),
                                             vmem_limit_bytes=VMEM_LIMIT),
        name="attn_fused",
    )(page_table, lam, qp, kp, vt, sza, subln_col, q3, knew3, vnew3, sza3, subln_row, ck2, cv2)


def _divmod(g, n):
    if isinstance(g, int):
        return divmod(g, n)
    return lax.div(g, jnp.int32(n)), lax.rem(g, jnp.int32(n))


def _decode_copies(refs, g, slot, *, pages, nc, page_rows):
    pt_ref, ck_hbm, cv_hbm, kbuf, vbuf, sem = refs
    seq, chunk = _divmod(g, nc)
    cps = []
    for p in range(pages):
        row0 = pl.multiple_of(pt_ref[seq, chunk * pages + p] * page_rows, page_rows)
        cps.append(pltpu.make_async_copy(ck_hbm.at[pl.ds(row0, page_rows), :], kbuf.at[slot, p], sem.at[0, slot]))
        cps.append(pltpu.make_async_copy(cv_hbm.at[pl.ds(row0, page_rows), :],
                                         vbuf.at[slot, pl.ds(p * page_rows, page_rows), :], sem.at[1, slot]))
    return cps


def _decode_parts(g, dma_refs, lam_ref, q_ref, knew_ref, vnew_ref, sza_ref, subln_ref, out_ref,
                  qbd_ref, m_ref, l_ref, acc_ref, *, pages, nc, page_rows, total):
    kbuf, vbuf = dma_refs[3], dma_refs[4]
    lq, d = q_ref.shape[1], q_ref.shape[2]
    nrow = 2 * N_HEADS * lq
    tc = pages * LANES
    active = g < total
    gc = jnp.minimum(g, total - 1)
    seq, c_id = _divmod(gc, nc)
    slot = lax.rem(gc, 2)
    copies = functools.partial(_decode_copies, dma_refs, pages=pages, nc=nc, page_rows=page_rows)

    def pre():
        @pl.when(g + 1 < total)
        def _():
            for cp in copies(g + 1, 1 - slot):
                cp.start()

        @pl.when(active & (c_id == 0))
        def _():
            qt = jnp.concatenate([q_ref[seq]] * (2 * N_HEADS), axis=0)
            rh = lax.broadcasted_iota(jnp.int32, (nrow, d), 0) // lq
            ch = lax.broadcasted_iota(jnp.int32, (nrow, d), 1) // HEAD_DIM
            qbd_ref[...] = jnp.where(rh == ch, qt, 0.0).astype(BF16)
            m_ref[...] = jnp.full_like(m_ref, -jnp.inf)
            l_ref[...] = jnp.zeros_like(l_ref)
            acc_ref[...] = jnp.zeros_like(acc_ref)

        @pl.when(active)
        def _():
            for cp in copies(g, slot):
                cp.wait()

    def softmax(s):
        m_prev = m_ref[...]
        m_new = jnp.maximum(m_prev, jnp.max(s, axis=1, keepdims=True))
        alpha = jnp.exp(m_prev - m_new)
        p = jnp.exp(s - m_new)
        l_ref[...] = alpha * l_ref[...] + jnp.sum(p, axis=1, keepdims=True)
        m_ref[...] = m_new
        return p.astype(BF16), alpha

    def accumulate(pb, alpha, values, heads):
        pv = jnp.dot(pb, values, preferred_element_type=F32)
        for i, h in enumerate(heads):
            rows = slice(2 * lq * h, 2 * lq * (h + 1))
            acc_ref[h] = alpha[rows, :] * acc_ref[h] + pv[rows, i * PAIR:(i + 1) * PAIR]

    def update(s, values):
        pb, alpha = softmax(s)
        accumulate(pb, alpha, values, range(N_HEADS))

    def scores():
        kt = jnp.concatenate([kbuf[slot, p] for p in range(pages)], axis=1).astype(BF16)
        return jnp.dot(qbd_ref[...], kt, preferred_element_type=F32)

    def cached_values(heads):
        return jnp.concatenate(
            [vbuf[slot, pl.ds(h, tc, stride=N_HEADS), :].astype(BF16) for h in heads], axis=1)

    def update_cached(s, pieces=1):
        state = {}
        per = N_HEADS // pieces

        def piece(i):
            if i == 0:
                state["pb"], state["alpha"] = softmax(s)
            heads = range(i * per, (i + 1) * per)
            accumulate(state["pb"], state["alpha"], cached_values(heads), heads)

        return [functools.partial(piece, i) for i in range(pieces)]

    def post():
        pl.when(active & (c_id == nc - 1))(finish)

    def finish():
        pad = jnp.zeros((LANES - lq, d), F32)
        kn = jnp.concatenate([knew_ref[seq], pad], axis=0).astype(BF16)
        vn = jnp.concatenate([vnew_ref[seq], pad], axis=0).astype(BF16)
        s2 = lax.dot_general(qbd_ref[...], kn, (((1,), (1,)), ((), ())), preferred_element_type=F32)
        qi = lax.rem(lax.broadcasted_iota(jnp.int32, (nrow, LANES), 0), lq)
        kj = lax.broadcasted_iota(jnp.int32, (nrow, LANES), 1)
        s2 = jnp.where(kj <= qi, s2, -jnp.inf)
        update(s2, vn)
        inv_l = 1.0 / l_ref[...]
        outs = []
        for h in range(N_HEADS):
            rows = slice(2 * lq * h, 2 * lq * (h + 1))
            o = acc_ref[h] * inv_l[rows, :]
            out = o[:lq] - lam_ref[0, 0] * o[lq:]
            ms = jnp.mean(out * out, axis=1, keepdims=True)
            outs.append(out * lax.rsqrt(ms + NORM_EPS) * subln_ref[...] * (1.0 - LAM_INIT))
        out_ref[seq] = jnp.concatenate(outs, axis=1) * sza_ref[seq]

    return pre, scores, update_cached, post


def _finish_kernel(a_ref, sga_ref, mp_ref, x_ref, wab_ref, wo_ref, g_ref, b_ref, y_ref):
    br_a = jnp.dot(a_ref[...].astype(BF16), wab_ref[...], preferred_element_type=F32)
    m = mp_ref[...] + sga_ref[...].astype(F32) * br_a
    out = jnp.dot(m.astype(BF16), wo_ref[...], preferred_element_type=F32)
    h = ALPHA * x_ref[...] + out
    mu = jnp.mean(h, axis=-1, keepdims=True)
    hc = h - mu
    var = jnp.mean(hc * hc, axis=-1, keepdims=True)
    y_ref[...] = hc * lax.rsqrt(var + NORM_EPS) * g_ref[...] + b_ref[...]


def _finish(a, sga, mp, x, wab, wo, ln_g, ln_b):
    rows, d = x.shape
    tm = min(FIN_TM, rows)
    row = lambda i: (i, 0)
    const = lambda i: (0, 0)
    return pl.pallas_call(
        _finish_kernel,
        grid=(rows // tm,),
        in_specs=[pl.BlockSpec((tm, d), row), pl.BlockSpec((tm, d), row), pl.BlockSpec((tm, d), row),
                  pl.BlockSpec((tm, d), row), pl.BlockSpec((d, d), const), pl.BlockSpec((d, d), const),
                  pl.BlockSpec((1, d), const), pl.BlockSpec((1, d), const)],
        out_specs=pl.BlockSpec((tm, d), row),
        out_shape=jax.ShapeDtypeStruct((rows, d), F32),
        compiler_params=pltpu.CompilerParams(dimension_semantics=("arbitrary",), vmem_limit_bytes=VMEM_LIMIT),
        name="finish",
    )(a, sga, mp, x, wab, wo, ln_g, ln_b)


def kernel(x_prompt, x_sample, cache_k, cache_v, state_pool, page_table, w_in, w_pool_mix, pool_scale, lambda_q1, lambda_k1, lambda_q2, lambda_k2, subln_w, w_pool_branch, w_attn_branch, w_o, ln_g, ln_b):
    assert w_in.shape[0] == DEPTH
    nb, L, d = x_prompt.shape
    ns, ls, _ = x_sample.shape
    n_phys, page_size = cache_k.shape[1], cache_k.shape[2]
    assert page_size == LANES
    past_len = page_table.shape[1] * page_size

    lam = (jnp.exp(jnp.sum(lambda_q1[0] * lambda_k1[0])) - jnp.exp(jnp.sum(lambda_q2[0] * lambda_k2[0]))
           + LAM_INIT).reshape(1, 1).astype(F32)
    w_in_b = w_in[0].astype(BF16)
    wmix_b = w_pool_mix[0].astype(BF16)
    wpb_b = w_pool_branch[0].astype(BF16)
    wab_b = w_attn_branch[0].astype(BF16)
    wo_b = w_o[0].astype(BF16)
    pscale = pool_scale[0].reshape(1, POOL_WIDTH)
    g_row = ln_g[0].reshape(1, d)
    b_row = ln_b[0].reshape(1, d)

    tabs_p = _rope_tables(jnp.arange(L))
    k_p, v_p, qp, kp, vt, sza, sga, mp, tail = _proj_prompt(x_prompt, tabs_p, w_in_b, wmix_b, pscale, wpb_b)
    pos_s = jnp.tile(past_len + jnp.arange(ls), ns)
    tabs_s = _rope_tables(pos_s)
    prev16 = jnp.pad(state_pool[0], ((0, 0), (16 - POOL_CTX, 0), (0, 0)))
    x_s2 = x_sample.reshape(ns * ls, d)
    k_s, v_s, q_s, sza_s, sga_s, mp_s, ext_s = _proj_sample(x_s2, tabs_s, prev16, w_in_b, wmix_b, pscale, wpb_b,
                                                            ns, ls, past_len)
    page_rows = page_size * d // LANES
    ck2 = jnp.transpose(cache_k[0], (0, 2, 3, 1)).reshape(n_phys * page_rows, LANES)
    cv2 = cache_v[0].reshape(n_phys * page_rows, LANES)
    r3 = lambda z: z.reshape(ns, ls, d)
    a_p, a_s = _attn_fused(page_table, lam, qp, kp, vt, sza, subln_w[0].reshape(PAIR, 1),
                           r3(q_s), r3(k_s), r3(v_s), r3(sza_s), subln_w[0].reshape(1, PAIR), ck2, cv2, page_rows)
    y_p = _finish(a_p.reshape(nb * L, d), sga.reshape(nb * L, d), mp.reshape(nb * L, d),
                  x_prompt.reshape(nb * L, d), wab_b, wo_b, g_row, b_row).reshape(nb, L, d)
    y_s = _finish(a_s.reshape(ns * ls, d), sga_s, mp_s, x_s2, wab_b, wo_b, g_row, b_row).reshape(ns, ls, d)

    return (y_p, y_s,
            k_p.reshape(1, nb, L, 2 * N_HEADS, HEAD_DIM), v_p.reshape(1, nb, L, N_HEADS, PAIR),
            tail[:, 16 - POOL_CTX:][None],
            k_s.reshape(1, ns, ls, 2 * N_HEADS, HEAD_DIM), v_s.reshape(1, ns, ls, N_HEADS, PAIR),
            ext_s[:, 16 + ls - POOL_CTX:][None])
```

```python
import functools
import math

import jax
import jax.numpy as jnp
from jax import lax
from jax.experimental import pallas as pl
from jax.experimental.pallas import tpu as pltpu

F32 = jnp.float32
BF16 = jnp.bfloat16

N_HEADS = 8
HEAD_DIM = 64
PAIR = 2 * HEAD_DIM
VT_ROWS = PAIR + 16
POOL_WINDOWS = (2, 4, 8, 16)
POOL_GC = 128
POOL_WIDTH = POOL_GC * len(POOL_WINDOWS)
POOL_CTX = max(POOL_WINDOWS) - 1
ROPE_THETA = 10000.0
NORM_EPS = 1e-5
DEPTH = 1
ALPHA = (2.0 * DEPTH) ** 0.25
LAM_INIT = 0.8 - 0.6 * math.exp(-0.3 * 0)
LOG2E = math.log2(math.e)

C_UP, C_ZP, C_Q, C_K, C_V, C_ZA, C_GP, C_GA, C_END = (0, 512, 1024, 2048, 3072, 4096, 5120, 6144, 7168)

LANES = 128
POOL_HIST = 32
VMEM_LIMIT = 56 * 1024 * 1024

PROJ_TM = 256
ATTN_T = 512
ATTN_LOOKAHEAD = 1
ATTN_HPS = 4
FIN_TM = 512
PAGES_PER_CHUNK = 8


def _silu(z):
    return z * jax.nn.sigmoid(z)


def _rope_block(x, c, sn, sp):
    return x * c + pltpu.roll(x, LANES - 32, axis=1) * sn + pltpu.roll(x, 32, axis=1) * sp


def _rope_tables(pos):
    inv_freq = 1.0 / (ROPE_THETA ** (jnp.arange(0, HEAD_DIM, 2, dtype=F32) / HEAD_DIM))
    ang = pos.astype(F32)[:, None] * inv_freq[None, :]
    cos = jnp.cos(ang)
    sin = jnp.sin(ang)
    z = jnp.zeros_like(sin)
    c = jnp.tile(cos, (1, 4))
    sn = jnp.tile(jnp.concatenate([-sin, z], axis=1), (1, 2))
    sp = jnp.tile(jnp.concatenate([z, sin], axis=1), (1, 2))
    return c, sn, sp


def _pool_mix(sums, u, pos, wmix_ref, pscale_ref):
    outs = []
    for g, w in enumerate(POOL_WINDOWS):
        cols = slice(g * POOL_GC, (g + 1) * POOL_GC)
        cnt = jnp.minimum(pos + 1, w).astype(F32)
        d = sums[g] / cnt - u[:, cols]
        mixed = jnp.dot(d.astype(BF16), wmix_ref[g], preferred_element_type=F32)
        outs.append(mixed * pscale_ref[:, cols])
    return jnp.concatenate(outs, axis=1)


def _proj_prompt_kernel(x_ref, c_ref, sn_ref, sp_ref, w_ref, wmix_ref, pscale_ref, wpb_ref,
                        k_ref, v_ref, qp_ref, kp_ref, vt_ref, sza_ref, sga_ref, mp_ref, tail_ref,
                        ext_ref, sb_ref, *, tm):
    t = pl.program_id(1)
    n = POOL_HIST + tm
    xb = x_ref[0].astype(BF16)

    def proj(lo, hi):
        return jnp.dot(xb, w_ref[:, lo:hi], preferred_element_type=F32)

    u = proj(C_UP, C_ZP)

    @pl.when(t == 0)
    def _():
        ext_ref[0:POOL_HIST, :] = jnp.zeros((POOL_HIST, POOL_WIDTH), F32)

    ext_ref[POOL_HIST:n, :] = u
    sb_ref[0, 8:n, :] = ext_ref[8:n, :] + ext_ref[7:n - 1, :]
    sb_ref[1, 16:n, 128:] = sb_ref[0, 16:n, 128:] + sb_ref[0, 14:n - 2, 128:]
    sb_ref[2, 24:n, 256:] = sb_ref[1, 24:n, 256:] + sb_ref[1, 20:n - 4, 256:]
    s16 = sb_ref[2, 32:n, 384:] + sb_ref[2, 24:n - 8, 384:]
    sums = [sb_ref[0, 32:n, 0:128], sb_ref[1, 32:n, 128:256], sb_ref[2, 32:n, 256:384], s16]
    tail_ref[0] = ext_ref[n - 16:n, :]
    ext_ref[0:POOL_HIST, :] = ext_ref[tm:n, :]

    pos = t * tm + lax.broadcasted_iota(jnp.int32, (tm, POOL_GC), 0)
    pool_o = _pool_mix(sums, u, pos, wmix_ref, pscale_ref)
    zp = proj(C_ZP, C_Q)
    br_p = jnp.dot((pool_o * _silu(zp)).astype(BF16), wpb_ref[...], preferred_element_type=F32)
    mp_ref[0] = jax.nn.sigmoid(proj(C_GP, C_GA)) * br_p

    c = c_ref[...]
    sn = sn_ref[...]
    sp = sp_ref[...]
    q = proj(C_Q, C_K) * (HEAD_DIM ** -0.5 * LOG2E)
    for h in range(N_HEADS):
        cols = slice(h * PAIR, (h + 1) * PAIR)
        qp_ref[0, h] = _rope_block(q[:, cols], c, sn, sp).astype(BF16)
    k = proj(C_K, C_V)
    for h in range(N_HEADS):
        cols = slice(h * PAIR, (h + 1) * PAIR)
        kr = _rope_block(k[:, cols], c, sn, sp)
        k_ref[0, :, cols] = kr
        kp_ref[0, h] = kr.astype(BF16)
    v = proj(C_V, C_ZA)
    v_ref[0] = v
    for h in range(N_HEADS):
        vt_ref[0, h, 0, 0:PAIR, :] = v[:, h * PAIR:(h + 1) * PAIR].T.astype(BF16)
        vt_ref[0, h, 0, PAIR:VT_ROWS, :] = jnp.ones((VT_ROWS - PAIR, tm), BF16)
    sza_ref[0] = _silu(proj(C_ZA, C_GP)).astype(BF16)
    sga_ref[0] = jax.nn.sigmoid(proj(C_GA, C_END)).astype(BF16)


def _proj_prompt(x, tabs, w_in, wmix, pscale, wpb):
    nb, L, d = x.shape
    tm = PROJ_TM
    nt = L // tm
    const = dict(pipeline_mode=pl.Buffered(1))
    row = lambda b, t: (b, t, 0)
    head = lambda b, t: (b, 0, t, 0)
    in_specs = [
        pl.BlockSpec((1, tm, d), row),
        pl.BlockSpec((tm, LANES), lambda b, t: (t, 0)),
        pl.BlockSpec((tm, LANES), lambda b, t: (t, 0)),
        pl.BlockSpec((tm, LANES), lambda b, t: (t, 0)),
        pl.BlockSpec((d, C_END), lambda b, t: (0, 0), **const),
        pl.BlockSpec((len(POOL_WINDOWS), POOL_GC, POOL_GC), lambda b, t: (0, 0, 0), **const),
        pl.BlockSpec((1, POOL_WIDTH), lambda b, t: (0, 0), **const),
        pl.BlockSpec((POOL_WIDTH, d), lambda b, t: (0, 0), **const),
    ]
    out_specs = [
        pl.BlockSpec((1, tm, d), row),
        pl.BlockSpec((1, tm, d), row),
        pl.BlockSpec((1, N_HEADS, tm, PAIR), head),
        pl.BlockSpec((1, N_HEADS, tm, PAIR), head),
        pl.BlockSpec((1, N_HEADS, 1, VT_ROWS, tm), lambda b, t: (b, 0, t, 0, 0)),
        pl.BlockSpec((1, tm, d), row),
        pl.BlockSpec((1, tm, d), row),
        pl.BlockSpec((1, tm, d), row),
        pl.BlockSpec((1, 16, POOL_WIDTH), lambda b, t: (b, 0, 0)),
    ]
    out_shape = [
        jax.ShapeDtypeStruct((nb, L, d), F32),
        jax.ShapeDtypeStruct((nb, L, d), F32),
        jax.ShapeDtypeStruct((nb, N_HEADS, L, PAIR), BF16),
        jax.ShapeDtypeStruct((nb, N_HEADS, L, PAIR), BF16),
        jax.ShapeDtypeStruct((nb, N_HEADS, nt, VT_ROWS, tm), BF16),
        jax.ShapeDtypeStruct((nb, L, d), BF16),
        jax.ShapeDtypeStruct((nb, L, d), BF16),
        jax.ShapeDtypeStruct((nb, L, d), F32),
        jax.ShapeDtypeStruct((nb, 16, POOL_WIDTH), F32),
    ]
    return pl.pallas_call(
        functools.partial(_proj_prompt_kernel, tm=tm),
        grid=(nb, nt),
        in_specs=in_specs,
        out_specs=out_specs,
        out_shape=out_shape,
        scratch_shapes=[pltpu.VMEM((POOL_HIST + tm, POOL_WIDTH), F32),
                        pltpu.VMEM((3, POOL_HIST + tm, POOL_WIDTH), F32)],
        compiler_params=pltpu.CompilerParams(dimension_semantics=("arbitrary", "arbitrary"),
                                             vmem_limit_bytes=VMEM_LIMIT),
        name="proj_prompt",
    )(x, *tabs, w_in, wmix, pscale, wpb)


def _proj_sample_kernel(x_ref, c_ref, sn_ref, sp_ref, prev_ref, w_ref, wmix_ref, pscale_ref, wpb_ref,
                        k_ref, v_ref, q_ref, sza_ref, sga_ref, mp_ref, ext_ref, *, ns, ls, pos0):
    rows = ns * ls
    xb = x_ref[...].astype(BF16)

    def proj(lo, hi):
        return jnp.dot(xb, w_ref[:, lo:hi], preferred_element_type=F32)

    u = proj(C_UP, C_ZP)
    ext_ref[:, 0:16, :] = prev_ref[...]
    ext_ref[:, 16:16 + ls, :] = u.reshape(ns, ls, POOL_WIDTH)
    sums = []
    for g, w in enumerate(POOL_WINDOWS):
        cols = slice(g * POOL_GC, (g + 1) * POOL_GC)
        s = ext_ref[:, 16:16 + ls, cols]
        for j in range(1, w):
            s = s + ext_ref[:, 16 - j:16 + ls - j, cols]
        sums.append(s.reshape(rows, POOL_GC))
    pos = pos0 + lax.broadcasted_iota(jnp.int32, (ns, ls, POOL_GC), 1).reshape(rows, POOL_GC)
    pool_o = _pool_mix(sums, u, pos, wmix_ref, pscale_ref)
    zp = proj(C_ZP, C_Q)
    br_p = jnp.dot((pool_o * _silu(zp)).astype(BF16), wpb_ref[...], preferred_element_type=F32)
    mp_ref[...] = jax.nn.sigmoid(proj(C_GP, C_GA)) * br_p

    c = c_ref[...]
    sn = sn_ref[...]
    sp = sp_ref[...]
    q = proj(C_Q, C_K) * (HEAD_DIM ** -0.5)
    k = proj(C_K, C_V)
    for h in range(N_HEADS):
        cols = slice(h * PAIR, (h + 1) * PAIR)
        q_ref[:, cols] = _rope_block(q[:, cols], c, sn, sp)
        k_ref[:, cols] = _rope_block(k[:, cols], c, sn, sp)
    v_ref[...] = proj(C_V, C_ZA)
    sza_ref[...] = _silu(proj(C_ZA, C_GP))
    sga_ref[...] = jax.nn.sigmoid(proj(C_GA, C_END))


def _proj_sample(x2, tabs, prev16, w_in, wmix, pscale, wpb, ns, ls, pos0):
    rows, d = x2.shape
    big = jax.ShapeDtypeStruct((rows, d), F32)
    return pl.pallas_call(
        functools.partial(_proj_sample_kernel, ns=ns, ls=ls, pos0=pos0),
        out_shape=[big, big, big, big, big, big, jax.ShapeDtypeStruct((ns, 16 + ls, POOL_WIDTH), F32)],
        compiler_params=pltpu.CompilerParams(vmem_limit_bytes=VMEM_LIMIT),
        name="proj_sample",
    )(x2, *tabs, prev16, w_in, wmix, pscale, wpb)


COL_REDUCE_ROWS = 64


def _col_reduce(x, op):
    rows, cols = x.shape
    slabs = op(x.reshape(rows // COL_REDUCE_ROWS, COL_REDUCE_ROWS, cols), axis=0)
    return op(slabs, axis=0, keepdims=True)


def _attn_units(kts, vts, ws, ms, ls, acc_ref, mask, between=None):
    between = between or {}

    def scores(u):
        s = lax.dot_general(kts[u], ws[u], (((1,), (1,)), ((), ())), preferred_element_type=F32)
        between.get(("scores", u), lambda: None)()
        return s

    n = len(ws)
    ms_new, ls_new = [], []
    pending = [scores(u) for u in range(min(ATTN_LOOKAHEAD, n))]
    for u in range(n):
        s = pending.pop(0)
        if u + ATTN_LOOKAHEAD < n:
            pending.append(scores(u + ATTN_LOOKAHEAD))
        if mask is not None:
            s = jnp.where(mask, s, -jnp.inf)
        m_new = jnp.maximum(ms[u], _col_reduce(s, jnp.max))
        alpha = jnp.exp2(ms[u] - m_new)
        p = jnp.exp2((s - m_new).astype(BF16))
        ms_new.append(m_new)
        pv = jnp.dot(vts[u], p, preferred_element_type=F32)
        acc_ref[u] = alpha * acc_ref[u] + pv[:PAIR]
        ls_new.append(alpha * ls[u] + pv[PAIR:PAIR + 1])
        between.get(("unit", u), lambda: None)()
    return tuple(ms_new), tuple(ls_new)


def _attn_fused_kernel(pt_ref, lam_ref, q_ref, k_ref, vt_ref, sza_ref, subln_ref,
                       qs_ref, knew_ref, vnew_ref, szas_ref, sublnr_ref, ck_hbm, cv_hbm,
                       out_ref, dout_ref,
                       acc_ref, kbuf, vbuf, sem, qbd_ref, dm_ref, dl_ref, dacc_ref, gcount_ref,
                       *, t, cv, hps, pages, nc, page_rows, total):
    j = pl.program_id(2)
    lin = (pl.program_id(0) * pl.num_programs(1) + pl.program_id(1)) * pl.num_programs(2) + j
    n_lin = pl.num_programs(0) * pl.num_programs(1) * pl.num_programs(2)
    dma_refs = (pt_ref, ck_hbm, cv_hbm, kbuf, vbuf, sem)
    dec_kw = dict(pages=pages, nc=nc, page_rows=page_rows)

    @pl.when(lin == 0)
    def _():
        gcount_ref[0] = 0
        for cp in _decode_copies(dma_refs, 0, 0, **dec_kw):
            cp.start()

    g0 = gcount_ref[0]

    def decode_parts(g):
        return _decode_parts(g, dma_refs, lam_ref, qs_ref, knew_ref, vnew_ref, szas_ref, sublnr_ref, dout_ref,
                             qbd_ref, dm_ref, dl_ref, dacc_ref, total=total, **dec_kw)

    lane = lax.broadcasted_iota(jnp.int32, (t, PAIR), 1)
    ws = []
    for u in range(hps):
        q = q_ref[0, u]
        zero = jnp.zeros_like(q)
        ws.append(jnp.concatenate([jnp.where(lane < HEAD_DIM, q, zero), jnp.where(lane >= HEAD_DIM, q, zero)],
                                  axis=0))
    acc_ref[...] = jnp.zeros_like(acc_ref)
    nck = t // cv

    def tile(i, ms, ls, mask):
        pre, dec_scores, dec_update, post = decode_parts(g0 + i)
        pre()
        kts = [k_ref[0, u, pl.ds(pl.multiple_of(i * t, t), t), :] for u in range(hps)]
        vts = [jnp.concatenate([vt_ref[0, u, i * nck + c] for c in range(nck)], axis=1) for u in range(hps)]
        dec_pieces = dec_update(pieces=hps)
        dec_pieces[0]()
        between = {("scores", u): piece for u, piece in enumerate(dec_pieces[1:])}
        ms, ls = _attn_units(kts, vts, ws, ms, ls, acc_ref, mask, between=between)
        post()
        return ms, ls

    m0 = tuple(jnp.full((1, 2 * t), -jnp.inf, F32) for _ in range(hps))
    l0 = tuple(jnp.zeros((1, 2 * t), F32) for _ in range(hps))
    ms, ls = lax.fori_loop(0, j, lambda i, c: tile(i, c[0], c[1], None), (m0, l0))
    krow = lax.broadcasted_iota(jnp.int32, (t, 2 * t), 0)
    qcol = lax.broadcasted_iota(jnp.int32, (t, 2 * t), 1)
    qcol = jnp.where(qcol >= t, qcol - t, qcol)
    ms, ls = tile(j, ms, ls, krow <= qcol)

    for u in range(hps):
        o = acc_ref[u] * (1.0 / ls[u])
        out = o[:, :t] - lam_ref[0, 0] * o[:, t:]
        msq = jnp.mean(out * out, axis=0, keepdims=True)
        out = out * lax.rsqrt(msq + NORM_EPS) * subln_ref[...] * (1.0 - LAM_INIT)
        cols = slice(u * PAIR, (u + 1) * PAIR)
        out_ref[0, :, cols] = (out.T * sza_ref[0, :, cols].astype(F32)).astype(BF16)

    g_next = g0 + j + 1
    gcount_ref[0] = g_next

    @pl.when(lin == n_lin - 1)
    def _():
        def body(g, carry):
            pre, dec_scores, dec_update, post = decode_parts(g)
            pre()
            for piece in dec_update():
                piece()
            post()
            return carry
        lax.fori_loop(g_next, total, body, 0)


def _attn_fused(page_table, lam, qp, kp, vt, sza, subln_col, q3, knew3, vnew3, sza3, subln_row, ck2, cv2,
                page_rows):
    nb, nh, L, _ = qp.shape
    ns, lq, d = q3.shape
    t = ATTN_T
    hps = ATTN_HPS
    nchunk, cv = vt.shape[2], vt.shape[4]
    n_pages = page_table.shape[1]
    pages = PAGES_PER_CHUNK
    assert n_pages % pages == 0
    nc = n_pages // pages
    nrow = 2 * N_HEADS * lq
    once = dict(pipeline_mode=pl.Buffered(1))
    whole = lambda b, h, j, pt: (0, 0, 0)
    grid_spec = pltpu.PrefetchScalarGridSpec(
        num_scalar_prefetch=1,
        grid=(nb, nh // hps, L // t),
        in_specs=[
            pl.BlockSpec(memory_space=pltpu.SMEM),
            pl.BlockSpec((1, hps, t, PAIR), lambda b, h, j, pt: (b, h, j, 0)),
            pl.BlockSpec((1, hps, L, PAIR), lambda b, h, j, pt: (b, h, 0, 0), **once),
            pl.BlockSpec((1, hps, nchunk, VT_ROWS, cv), lambda b, h, j, pt: (b, h, 0, 0, 0), **once),
            pl.BlockSpec((1, t, hps * PAIR), lambda b, h, j, pt: (b, j, h)),
            pl.BlockSpec((PAIR, 1), lambda b, h, j, pt: (0, 0)),
            pl.BlockSpec((ns, lq, d), whole, **once),
            pl.BlockSpec((ns, lq, d), whole, **once),
            pl.BlockSpec((ns, lq, d), whole, **once),
            pl.BlockSpec((ns, lq, d), whole, **once),
            pl.BlockSpec((1, PAIR), lambda b, h, j, pt: (0, 0)),
            pl.BlockSpec(memory_space=pl.ANY),
            pl.BlockSpec(memory_space=pl.ANY),
        ],
        out_specs=[
            pl.BlockSpec((1, t, hps * PAIR), lambda b, h, j, pt: (b, j, h)),
            pl.BlockSpec((ns, lq, d), whole),
        ],
        scratch_shapes=[
            pltpu.VMEM((hps, PAIR, 2 * t), F32),
            pltpu.VMEM((2, pages, page_rows, LANES), F32),
            pltpu.VMEM((2, pages * page_rows, LANES), F32),
            pltpu.SemaphoreType.DMA((2, 2)),
            pltpu.VMEM((nrow, d), BF16),
            pltpu.VMEM((nrow, 1), F32),
            pltpu.VMEM((nrow, 1), F32),
            pltpu.VMEM((N_HEADS, 2 * lq, PAIR), F32),
            pltpu.SMEM((1,), jnp.int32),
        ],
    )
    return pl.pallas_call(
        functools.partial(_attn_fused_kernel, t=t, cv=cv, hps=hps, pages=pages, nc=nc, page_rows=page_rows,
                          total=ns * nc),
        grid_spec=grid_spec,
        out_shape=[jax.ShapeDtypeStruct((nb, L, nh * PAIR), BF16), jax.ShapeDtypeStruct((ns, lq, d), F32)],
        compiler_params=pltpu.CompilerParams(dimension_semantics=("arbitrary", "arbitrary", "arbitrary"),
                                             vmem_limit_bytes=VMEM_LIMIT),
        name="attn_fused",
    )(page_table, lam, qp, kp, vt, sza, subln_col, q3, knew3, vnew3, sza3, subln_row, ck2, cv2)


def _divmod(g, n):
    if isinstance(g, int):
        return divmod(g, n)
    return lax.div(g, jnp.int32(n)), lax.rem(g, jnp.int32(n))


def _decode_copies(refs, g, slot, *, pages, nc, page_rows):
    pt_ref, ck_hbm, cv_hbm, kbuf, vbuf, sem = refs
    seq, chunk = _divmod(g, nc)
    cps = []
    for p in range(pages):
        row0 = pl.multiple_of(pt_ref[seq, chunk * pages + p] * page_rows, page_rows)
        cps.append(pltpu.make_async_copy(ck_hbm.at[pl.ds(row0, page_rows), :], kbuf.at[slot, p], sem.at[0, slot]))
        cps.append(pltpu.make_async_copy(cv_hbm.at[pl.ds(row0, page_rows), :],
                                         vbuf.at[slot, pl.ds(p * page_rows, page_rows), :], sem.at[1, slot]))
    return cps


def _decode_parts(g, dma_refs, lam_ref, q_ref, knew_ref, vnew_ref, sza_ref, subln_ref, out_ref,
                  qbd_ref, m_ref, l_ref, acc_ref, *, pages, nc, page_rows, total):
    kbuf, vbuf = dma_refs[3], dma_refs[4]
    lq, d = q_ref.shape[1], q_ref.shape[2]
    nrow = 2 * N_HEADS * lq
    tc = pages * LANES
    active = g < total
    gc = jnp.minimum(g, total - 1)
    seq, c_id = _divmod(gc, nc)
    slot = lax.rem(gc, 2)
    copies = functools.partial(_decode_copies, dma_refs, pages=pages, nc=nc, page_rows=page_rows)

    def pre():
        @pl.when(g + 1 < total)
        def _():
            for cp in copies(g + 1, 1 - slot):
                cp.start()

        @pl.when(active & (c_id == 0))
        def _():
            qt = jnp.concatenate([q_ref[seq]] * (2 * N_HEADS), axis=0)
            rh = lax.broadcasted_iota(jnp.int32, (nrow, d), 0) // lq
            ch = lax.broadcasted_iota(jnp.int32, (nrow, d), 1) // HEAD_DIM
            qbd_ref[...] = jnp.where(rh == ch, qt, 0.0).astype(BF16)
            m_ref[...] = jnp.full_like(m_ref, -jnp.inf)
            l_ref[...] = jnp.zeros_like(l_ref)
            acc_ref[...] = jnp.zeros_like(acc_ref)

        @pl.when(active)
        def _():
            for cp in copies(g, slot):
                cp.wait()

    def softmax(s):
        m_prev = m_ref[...]
        m_new = jnp.maximum(m_prev, jnp.max(s, axis=1, keepdims=True))
        alpha = jnp.exp(m_prev - m_new)
        p = jnp.exp(s - m_new)
        l_ref[...] = alpha * l_ref[...] + jnp.sum(p, axis=1, keepdims=True)
        m_ref[...] = m_new
        return p.astype(BF16), alpha

    def accumulate(pb, alpha, values, heads):
        pv = jnp.dot(pb, values, preferred_element_type=F32)
        for i, h in enumerate(heads):
            rows = slice(2 * lq * h, 2 * lq * (h + 1))
            acc_ref[h] = alpha[rows, :] * acc_ref[h] + pv[rows, i * PAIR:(i + 1) * PAIR]

    def update(s, values):
        pb, alpha = softmax(s)
        accumulate(pb, alpha, values, range(N_HEADS))

    def scores():
        kt = jnp.concatenate([kbuf[slot, p] for p in range(pages)], axis=1).astype(BF16)
        return jnp.dot(qbd_ref[...], kt, preferred_element_type=F32)

    def cached_values(heads):
        return jnp.concatenate(
            [vbuf[slot, pl.ds(h, tc, stride=N_HEADS), :].astype(BF16) for h in heads], axis=1)

    def update_cached(pieces=1):
        state = {}
        per = N_HEADS // pieces

        def score_piece():
            state["s"] = scores()

        def piece(i):
            if i == 0:
                state["pb"], state["alpha"] = softmax(state["s"])
            heads = range(i * per, (i + 1) * per)
            accumulate(state["pb"], state["alpha"], cached_values(heads), heads)

        return [score_piece] + [functools.partial(piece, i) for i in range(pieces)]

    def post():
        pl.when(active & (c_id == nc - 1))(finish)

    def finish():
        pad = jnp.zeros((LANES - lq, d), F32)
        kn = jnp.concatenate([knew_ref[seq], pad], axis=0).astype(BF16)
        vn = jnp.concatenate([vnew_ref[seq], pad], axis=0).astype(BF16)
        s2 = lax.dot_general(qbd_ref[...], kn, (((1,), (1,)), ((), ())), preferred_element_type=F32)
        qi = lax.rem(lax.broadcasted_iota(jnp.int32, (nrow, LANES), 0), lq)
        kj = lax.broadcasted_iota(jnp.int32, (nrow, LANES), 1)
        s2 = jnp.where(kj <= qi, s2, -jnp.inf)
        update(s2, vn)
        inv_l = 1.0 / l_ref[...]
        outs = []
        for h in range(N_HEADS):
            rows = slice(2 * lq * h, 2 * lq * (h + 1))
            o = acc_ref[h] * inv_l[rows, :]
            out = o[:lq] - lam_ref[0, 0] * o[lq:]
            ms = jnp.mean(out * out, axis=1, keepdims=True)
            outs.append(out * lax.rsqrt(ms + NORM_EPS) * subln_ref[...] * (1.0 - LAM_INIT))
        out_ref[seq] = jnp.concatenate(outs, axis=1) * sza_ref[seq]

    return pre, scores, update_cached, post


def _finish_kernel(a_ref, sga_ref, mp_ref, x_ref, wab_ref, wo_ref, g_ref, b_ref, y_ref):
    br_a = jnp.dot(a_ref[...].astype(BF16), wab_ref[...], preferred_element_type=F32)
    m = mp_ref[...] + sga_ref[...].astype(F32) * br_a
    out = jnp.dot(m.astype(BF16), wo_ref[...], preferred_element_type=F32)
    h = ALPHA * x_ref[...] + out
    mu = jnp.mean(h, axis=-1, keepdims=True)
    hc = h - mu
    var = jnp.mean(hc * hc, axis=-1, keepdims=True)
    y_ref[...] = hc * lax.rsqrt(var + NORM_EPS) * g_ref[...] + b_ref[...]


def _finish(a, sga, mp, x, wab, wo, ln_g, ln_b):
    rows, d = x.shape
    tm = min(FIN_TM, rows)
    row = lambda i: (i, 0)
    const = lambda i: (0, 0)
    return pl.pallas_call(
        _finish_kernel,
        grid=(rows // tm,),
        in_specs=[pl.BlockSpec((tm, d), row), pl.BlockSpec((tm, d), row), pl.BlockSpec((tm, d), row),
                  pl.BlockSpec((tm, d), row), pl.BlockSpec((d, d), const), pl.BlockSpec((d, d), const),
                  pl.BlockSpec((1, d), const), pl.BlockSpec((1, d), const)],
        out_specs=pl.BlockSpec((tm, d), row),
        out_shape=jax.ShapeDtypeStruct((rows, d), F32),
        compiler_params=pltpu.CompilerParams(dimension_semantics=("arbitrary",), vmem_limit_bytes=VMEM_LIMIT),
        name="finish",
    )(a, sga, mp, x, wab, wo, ln_g, ln_b)


def kernel(x_prompt, x_sample, cache_k, cache_v, state_pool, page_table, w_in, w_pool_mix, pool_scale, lambda_q1, lambda_k1, lambda_q2, lambda_k2, subln_w, w_pool_branch, w_attn_branch, w_o, ln_g, ln_b):
    assert w_in.shape[0] == DEPTH
    nb, L, d = x_prompt.shape
    ns, ls, _ = x_sample.shape
    n_phys, page_size = cache_k.shape[1], cache_k.shape[2]
    assert page_size == LANES
    past_len = page_table.shape[1] * page_size

    lam = (jnp.exp(jnp.sum(lambda_q1[0] * lambda_k1[0])) - jnp.exp(jnp.sum(lambda_q2[0] * lambda_k2[0]))
           + LAM_INIT).reshape(1, 1).astype(F32)
    w_in_b = w_in[0].astype(BF16)
    wmix_b = w_pool_mix[0].astype(BF16)
    wpb_b = w_pool_branch[0].astype(BF16)
    wab_b = w_attn_branch[0].astype(BF16)
    wo_b = w_o[0].astype(BF16)
    pscale = pool_scale[0].reshape(1, POOL_WIDTH)
    g_row = ln_g[0].reshape(1, d)
    b_row = ln_b[0].reshape(1, d)

    tabs_p = _rope_tables(jnp.arange(L))
    k_p, v_p, qp, kp, vt, sza, sga, mp, tail = _proj_prompt(x_prompt, tabs_p, w_in_b, wmix_b, pscale, wpb_b)
    pos_s = jnp.tile(past_len + jnp.arange(ls), ns)
    tabs_s = _rope_tables(pos_s)
    prev16 = jnp.pad(state_pool[0], ((0, 0), (16 - POOL_CTX, 0), (0, 0)))
    x_s2 = x_sample.reshape(ns * ls, d)
    k_s, v_s, q_s, sza_s, sga_s, mp_s, ext_s = _proj_sample(x_s2, tabs_s, prev16, w_in_b, wmix_b, pscale, wpb_b,
                                                            ns, ls, past_len)
    page_rows = page_size * d // LANES
    ck2 = jnp.transpose(cache_k[0], (0, 2, 3, 1)).reshape(n_phys * page_rows, LANES)
    cv2 = cache_v[0].reshape(n_phys * page_rows, LANES)
    r3 = lambda z: z.reshape(ns, ls, d)
    a_p, a_s = _attn_fused(page_table, lam, qp, kp, vt, sza, subln_w[0].reshape(PAIR, 1),
                           r3(q_s), r3(k_s), r3(v_s), r3(sza_s), subln_w[0].reshape(1, PAIR), ck2, cv2, page_rows)
    y_p = _finish(a_p.reshape(nb * L, d), sga.reshape(nb * L, d), mp.reshape(nb * L, d),
                  x_prompt.reshape(nb * L, d), wab_b, wo_b, g_row, b_row).reshape(nb, L, d)
    y_s = _finish(a_s.reshape(ns * ls, d), sga_s, mp_s, x_s2, wab_b, wo_b, g_row, b_row).reshape(ns, ls, d)

    return (y_p, y_s,
            k_p.reshape(1, nb, L, 2 * N_HEADS, HEAD_DIM), v_p.reshape(1, nb, L, N_HEADS, PAIR),
            tail[:, 16 - POOL_CTX:][None],
            k_s.reshape(1, ns, ls, 2 * N_HEADS, HEAD_DIM), v_s.reshape(1, ns, ls, N_HEADS, PAIR),
            ext_s[:, 16 + ls - POOL_CTX:][None])
```

```python
import functools
import math

import jax
import jax.numpy as jnp
from jax import lax
from jax.experimental import pallas as pl
from jax.experimental.pallas import tpu as pltpu

F32 = jnp.float32
BF16 = jnp.bfloat16

N_HEADS = 8
HEAD_DIM = 64
PAIR = 2 * HEAD_DIM
VT_ROWS = PAIR + 16
POOL_WINDOWS = (2, 4, 8, 16)
POOL_GC = 128
POOL_WIDTH = POOL_GC * len(POOL_WINDOWS)
POOL_CTX = max(POOL_WINDOWS) - 1
ROPE_THETA = 10000.0
NORM_EPS = 1e-5
DEPTH = 1
ALPHA = (2.0 * DEPTH) ** 0.25
LAM_INIT = 0.8 - 0.6 * math.exp(-0.3 * 0)
LOG2E = math.log2(math.e)

C_UP, C_ZP, C_Q, C_K, C_V, C_ZA, C_GP, C_GA, C_END = (0, 512, 1024, 2048, 3072, 4096, 5120, 6144, 7168)

LANES = 128
POOL_HIST = 32
VMEM_LIMIT = 56 * 1024 * 1024

PROJ_TM = 256
ATTN_T = 512
ATTN_LOOKAHEAD = 1
ATTN_HPS = 4
FIN_TM = 512
PAGES_PER_CHUNK = 8


def _silu(z):
    return z * jax.nn.sigmoid(z)


def _rope_block(x, c, sn, sp):
    return x * c + pltpu.roll(x, LANES - 32, axis=1) * sn + pltpu.roll(x, 32, axis=1) * sp


def _rope_tables(pos):
    inv_freq = 1.0 / (ROPE_THETA ** (jnp.arange(0, HEAD_DIM, 2, dtype=F32) / HEAD_DIM))
    ang = pos.astype(F32)[:, None] * inv_freq[None, :]
    cos = jnp.cos(ang)
    sin = jnp.sin(ang)
    z = jnp.zeros_like(sin)
    c = jnp.tile(cos, (1, 4))
    sn = jnp.tile(jnp.concatenate([-sin, z], axis=1), (1, 2))
    sp = jnp.tile(jnp.concatenate([z, sin], axis=1), (1, 2))
    return c, sn, sp


def _pool_mix(sums, u, pos, wmix_ref, pscale_ref):
    outs = []
    for g, w in enumerate(POOL_WINDOWS):
        cols = slice(g * POOL_GC, (g + 1) * POOL_GC)
        cnt = jnp.minimum(pos + 1, w).astype(F32)
        d = sums[g] / cnt - u[:, cols]
        mixed = jnp.dot(d.astype(BF16), wmix_ref[g], preferred_element_type=F32)
        outs.append(mixed * pscale_ref[:, cols])
    return jnp.concatenate(outs, axis=1)


def _proj_prompt_kernel(x_ref, c_ref, sn_ref, sp_ref, w_ref, wmix_ref, pscale_ref, wpb_ref,
                        k_ref, v_ref, qp_ref, kp_ref, vt_ref, sza_ref, sga_ref, mp_ref, tail_ref,
                        ext_ref, sb_ref, *, tm):
    t = pl.program_id(1)
    n = POOL_HIST + tm

    @pl.when(t == 0)
    def _():
        ext_ref[0:POOL_HIST, :] = jnp.zeros((POOL_HIST, POOL_WIDTH), F32)

    xb = x_ref[0].astype(BF16)

    def proj(lo, hi):
        return jnp.dot(xb, w_ref[:, lo:hi], preferred_element_type=F32)

    u = proj(C_UP, C_ZP)
    ext_ref[POOL_HIST:n, :] = u
    sb_ref[0, 8:n, :] = ext_ref[8:n, :] + ext_ref[7:n - 1, :]
    sb_ref[1, 16:n, 128:] = sb_ref[0, 16:n, 128:] + sb_ref[0, 14:n - 2, 128:]
    sb_ref[2, 24:n, 256:] = sb_ref[1, 24:n, 256:] + sb_ref[1, 20:n - 4, 256:]
    s16 = sb_ref[2, 32:n, 384:] + sb_ref[2, 24:n - 8, 384:]
    sums = [sb_ref[0, 32:n, 0:128], sb_ref[1, 32:n, 128:256], sb_ref[2, 32:n, 256:384], s16]
    tail_ref[0] = ext_ref[n - 16:n, :]
    ext_ref[0:POOL_HIST, :] = ext_ref[tm:n, :]

    c = c_ref[...]
    sn = sn_ref[...]
    sp = sp_ref[...]
    q = proj(C_Q, C_K) * (HEAD_DIM ** -0.5 * LOG2E)
    for h in range(N_HEADS):
        cols = slice(h * PAIR, (h + 1) * PAIR)
        qp_ref[0, h] = _rope_block(q[:, cols], c, sn, sp).astype(BF16)
    k = proj(C_K, C_V)
    for h in range(N_HEADS):
        cols = slice(h * PAIR, (h + 1) * PAIR)
        kr = _rope_block(k[:, cols], c, sn, sp)
        k_ref[0, :, cols] = kr
        kp_ref[0, h] = kr.astype(BF16)
    v = proj(C_V, C_ZA)
    v_ref[0] = v
    for h in range(N_HEADS):
        vt_ref[0, h, 0, 0:PAIR, :] = v[:, h * PAIR:(h + 1) * PAIR].T.astype(BF16)
        vt_ref[0, h, 0, PAIR:VT_ROWS, :] = jnp.ones((VT_ROWS - PAIR, tm), BF16)
    sza_ref[0] = _silu(proj(C_ZA, C_GP)).astype(BF16)
    sga_ref[0] = jax.nn.sigmoid(proj(C_GA, C_END)).astype(BF16)

    pos = t * tm + lax.broadcasted_iota(jnp.int32, (tm, POOL_GC), 0)
    pool_o = _pool_mix(sums, u, pos, wmix_ref, pscale_ref)
    zp = proj(C_ZP, C_Q)
    br_p = jnp.dot((pool_o * _silu(zp)).astype(BF16), wpb_ref[...], preferred_element_type=F32)
    mp_ref[0] = jax.nn.sigmoid(proj(C_GP, C_GA)) * br_p


def _proj_prompt(x, tabs, w_in, wmix, pscale, wpb):
    nb, L, d = x.shape
    tm = PROJ_TM
    nt = L // tm
    const = dict(pipeline_mode=pl.Buffered(1))
    row = lambda b, t: (b, t, 0)
    head = lambda b, t: (b, 0, t, 0)
    in_specs = [
        pl.BlockSpec((1, tm, d), row),
        pl.BlockSpec((tm, LANES), lambda b, t: (t, 0)),
        pl.BlockSpec((tm, LANES), lambda b, t: (t, 0)),
        pl.BlockSpec((tm, LANES), lambda b, t: (t, 0)),
        pl.BlockSpec((d, C_END), lambda b, t: (0, 0), **const),
        pl.BlockSpec((len(POOL_WINDOWS), POOL_GC, POOL_GC), lambda b, t: (0, 0, 0), **const),
        pl.BlockSpec((1, POOL_WIDTH), lambda b, t: (0, 0), **const),
        pl.BlockSpec((POOL_WIDTH, d), lambda b, t: (0, 0), **const),
    ]
    out_specs = [
        pl.BlockSpec((1, tm, d), row),
        pl.BlockSpec((1, tm, d), row),
        pl.BlockSpec((1, N_HEADS, tm, PAIR), head),
        pl.BlockSpec((1, N_HEADS, tm, PAIR), head),
        pl.BlockSpec((1, N_HEADS, 1, VT_ROWS, tm), lambda b, t: (b, 0, t, 0, 0)),
        pl.BlockSpec((1, tm, d), row),
        pl.BlockSpec((1, tm, d), row),
        pl.BlockSpec((1, tm, d), row),
        pl.BlockSpec((1, 16, POOL_WIDTH), lambda b, t: (b, 0, 0)),
    ]
    out_shape = [
        jax.ShapeDtypeStruct((nb, L, d), F32),
        jax.ShapeDtypeStruct((nb, L, d), F32),
        jax.ShapeDtypeStruct((nb, N_HEADS, L, PAIR), BF16),
        jax.ShapeDtypeStruct((nb, N_HEADS, L, PAIR), BF16),
        jax.ShapeDtypeStruct((nb, N_HEADS, nt, VT_ROWS, tm), BF16),
        jax.ShapeDtypeStruct((nb, L, d), BF16),
        jax.ShapeDtypeStruct((nb, L, d), BF16),
        jax.ShapeDtypeStruct((nb, L, d), F32),
        jax.ShapeDtypeStruct((nb, 16, POOL_WIDTH), F32),
    ]
    return pl.pallas_call(
        functools.partial(_proj_prompt_kernel, tm=tm),
        grid=(nb, nt),
        in_specs=in_specs,
        out_specs=out_specs,
        out_shape=out_shape,
        scratch_shapes=[pltpu.VMEM((POOL_HIST + tm, POOL_WIDTH), F32),
                        pltpu.VMEM((3, POOL_HIST + tm, POOL_WIDTH), F32)],
        compiler_params=pltpu.CompilerParams(dimension_semantics=("arbitrary", "arbitrary"),
                                             vmem_limit_bytes=VMEM_LIMIT),
        name="proj_prompt",
    )(x, *tabs, w_in, wmix, pscale, wpb)


def _proj_sample_kernel(x_ref, c_ref, sn_ref, sp_ref, prev_ref, w_ref, wmix_ref, pscale_ref, wpb_ref,
                        k_ref, v_ref, q_ref, sza_ref, sga_ref, mp_ref, ext_ref, *, ns, ls, pos0):
    rows = ns * ls
    xb = x_ref[...].astype(BF16)

    def proj(lo, hi):
        return jnp.dot(xb, w_ref[:, lo:hi], preferred_element_type=F32)

    u = proj(C_UP, C_ZP)
    ext_ref[:, 0:16, :] = prev_ref[...]
    ext_ref[:, 16:16 + ls, :] = u.reshape(ns, ls, POOL_WIDTH)
    sums = []
    for g, w in enumerate(POOL_WINDOWS):
        cols = slice(g * POOL_GC, (g + 1) * POOL_GC)
        s = ext_ref[:, 16:16 + ls, cols]
        for j in range(1, w):
            s = s + ext_ref[:, 16 - j:16 + ls - j, cols]
        sums.append(s.reshape(rows, POOL_GC))
    pos = pos0 + lax.broadcasted_iota(jnp.int32, (ns, ls, POOL_GC), 1).reshape(rows, POOL_GC)
    pool_o = _pool_mix(sums, u, pos, wmix_ref, pscale_ref)
    zp = proj(C_ZP, C_Q)
    br_p = jnp.dot((pool_o * _silu(zp)).astype(BF16), wpb_ref[...], preferred_element_type=F32)
    mp_ref[...] = jax.nn.sigmoid(proj(C_GP, C_GA)) * br_p

    c = c_ref[...]
    sn = sn_ref[...]
    sp = sp_ref[...]
    q = proj(C_Q, C_K) * (HEAD_DIM ** -0.5)
    k = proj(C_K, C_V)
    for h in range(N_HEADS):
        cols = slice(h * PAIR, (h + 1) * PAIR)
        q_ref[:, cols] = _rope_block(q[:, cols], c, sn, sp)
        k_ref[:, cols] = _rope_block(k[:, cols], c, sn, sp)
    v_ref[...] = proj(C_V, C_ZA)
    sza_ref[...] = _silu(proj(C_ZA, C_GP))
    sga_ref[...] = jax.nn.sigmoid(proj(C_GA, C_END))


def _proj_sample(x2, tabs, prev16, w_in, wmix, pscale, wpb, ns, ls, pos0):
    rows, d = x2.shape
    big = jax.ShapeDtypeStruct((rows, d), F32)
    return pl.pallas_call(
        functools.partial(_proj_sample_kernel, ns=ns, ls=ls, pos0=pos0),
        out_shape=[big, big, big, big, big, big, jax.ShapeDtypeStruct((ns, 16 + ls, POOL_WIDTH), F32)],
        compiler_params=pltpu.CompilerParams(vmem_limit_bytes=VMEM_LIMIT),
        name="proj_sample",
    )(x2, *tabs, prev16, w_in, wmix, pscale, wpb)


COL_REDUCE_ROWS = 64


def _col_reduce(x, op):
    rows, cols = x.shape
    slabs = op(x.reshape(rows // COL_REDUCE_ROWS, COL_REDUCE_ROWS, cols), axis=0)
    return op(slabs, axis=0, keepdims=True)


def _attn_units(kts, vts, ws, ms, ls, acc_ref, mask, between=None):
    between = between or {}

    def scores(u):
        s = lax.dot_general(kts[u], ws[u], (((1,), (1,)), ((), ())), preferred_element_type=F32)
        between.get(("scores", u), lambda: None)()
        return s

    n = len(ws)
    ms_new, ls_new = [], []
    pending = [scores(u) for u in range(min(ATTN_LOOKAHEAD, n))]
    for u in range(n):
        s = pending.pop(0)
        if u + ATTN_LOOKAHEAD < n:
            pending.append(scores(u + ATTN_LOOKAHEAD))
        if mask is not None:
            s = jnp.where(mask, s, -jnp.inf)
        m_new = jnp.maximum(ms[u], _col_reduce(s, jnp.max))
        alpha = jnp.exp2(ms[u] - m_new)
        p = jnp.exp2(s - m_new)
        ms_new.append(m_new)
        pv = jnp.dot(vts[u], p.astype(BF16), preferred_element_type=F32)
        acc_ref[u] = alpha * acc_ref[u] + pv[:PAIR]
        ls_new.append(alpha * ls[u] + pv[PAIR:PAIR + 1])
        between.get(("unit", u), lambda: None)()
    return tuple(ms_new), tuple(ls_new)


def _attn_fused_kernel(pt_ref, lam_ref, q_ref, k_ref, vt_ref, sza_ref, subln_ref,
                       qs_ref, knew_ref, vnew_ref, szas_ref, sublnr_ref, ck_hbm, cv_hbm,
                       out_ref, dout_ref,
                       acc_ref, kbuf, vbuf, sem, qbd_ref, dm_ref, dl_ref, dacc_ref, gcount_ref,
                       *, t, cv, hps, pages, nc, page_rows, total):
    j = pl.program_id(2)
    lin = (pl.program_id(0) * pl.num_programs(1) + pl.program_id(1)) * pl.num_programs(2) + j
    n_lin = pl.num_programs(0) * pl.num_programs(1) * pl.num_programs(2)
    dma_refs = (pt_ref, ck_hbm, cv_hbm, kbuf, vbuf, sem)
    dec_kw = dict(pages=pages, nc=nc, page_rows=page_rows)

    @pl.when(lin == 0)
    def _():
        gcount_ref[0] = 0
        for cp in _decode_copies(dma_refs, 0, 0, **dec_kw):
            cp.start()

    g0 = gcount_ref[0]

    def decode_parts(g):
        return _decode_parts(g, dma_refs, lam_ref, qs_ref, knew_ref, vnew_ref, szas_ref, sublnr_ref, dout_ref,
                             qbd_ref, dm_ref, dl_ref, dacc_ref, total=total, **dec_kw)

    lane = lax.broadcasted_iota(jnp.int32, (t, PAIR), 1)
    ws = []
    for u in range(hps):
        q = q_ref[0, u]
        zero = jnp.zeros_like(q)
        ws.append(jnp.concatenate([jnp.where(lane < HEAD_DIM, q, zero), jnp.where(lane >= HEAD_DIM, q, zero)],
                                  axis=0))
    acc_ref[...] = jnp.zeros_like(acc_ref)
    nck = t // cv

    def tile(i, ms, ls, mask):
        pre, dec_scores, dec_update, post = decode_parts(g0 + i)
        pre()
        kts = [k_ref[0, u, pl.ds(pl.multiple_of(i * t, t), t), :] for u in range(hps)]
        vts = [jnp.concatenate([vt_ref[0, u, i * nck + c] for c in range(nck)], axis=1) for u in range(hps)]
        dec_pieces = dec_update(pieces=hps)
        dec_pieces[0]()
        between = {("scores", u): piece for u, piece in enumerate(dec_pieces[1:])}
        ms, ls = _attn_units(kts, vts, ws, ms, ls, acc_ref, mask, between=between)
        post()
        return ms, ls

    m0 = tuple(jnp.full((1, 2 * t), -jnp.inf, F32) for _ in range(hps))
    l0 = tuple(jnp.zeros((1, 2 * t), F32) for _ in range(hps))
    ms, ls = lax.fori_loop(0, j, lambda i, c: tile(i, c[0], c[1], None), (m0, l0))
    krow = lax.broadcasted_iota(jnp.int32, (t, 2 * t), 0)
    qcol = lax.broadcasted_iota(jnp.int32, (t, 2 * t), 1)
    qcol = jnp.where(qcol >= t, qcol - t, qcol)
    ms, ls = tile(j, ms, ls, krow <= qcol)

    for u in range(hps):
        o = acc_ref[u] * (1.0 / ls[u])
        out = o[:, :t] - lam_ref[0, 0] * o[:, t:]
        msq = jnp.mean(out * out, axis=0, keepdims=True)
        out = out * lax.rsqrt(msq + NORM_EPS) * subln_ref[...] * (1.0 - LAM_INIT)
        cols = slice(u * PAIR, (u + 1) * PAIR)
        out_ref[0, :, cols] = (out.T * sza_ref[0, :, cols].astype(F32)).astype(BF16)

    g_next = g0 + j + 1
    gcount_ref[0] = g_next

    @pl.when(lin == n_lin - 1)
    def _():
        def body(g, carry):
            pre, dec_scores, dec_update, post = decode_parts(g)
            pre()
            for piece in dec_update():
                piece()
            post()
            return carry
        lax.fori_loop(g_next, total, body, 0)


def _attn_fused(page_table, lam, qp, kp, vt, sza, subln_col, q3, knew3, vnew3, sza3, subln_row, ck2, cv2,
                page_rows):
    nb, nh, L, _ = qp.shape
    ns, lq, d = q3.shape
    t = ATTN_T
    hps = ATTN_HPS
    nchunk, cv = vt.shape[2], vt.shape[4]
    n_pages = page_table.shape[1]
    pages = PAGES_PER_CHUNK
    assert n_pages % pages == 0
    nc = n_pages // pages
    nrow = 2 * N_HEADS * lq
    once = dict(pipeline_mode=pl.Buffered(1))
    whole = lambda b, h, j, pt: (0, 0, 0)
    grid_spec = pltpu.PrefetchScalarGridSpec(
        num_scalar_prefetch=1,
        grid=(nb, nh // hps, L // t),
        in_specs=[
            pl.BlockSpec(memory_space=pltpu.SMEM),
            pl.BlockSpec((1, hps, t, PAIR), lambda b, h, j, pt: (b, h, j, 0)),
            pl.BlockSpec((1, hps, L, PAIR), lambda b, h, j, pt: (b, h, 0, 0), **once),
            pl.BlockSpec((1, hps, nchunk, VT_ROWS, cv), lambda b, h, j, pt: (b, h, 0, 0, 0), **once),
            pl.BlockSpec((1, t, hps * PAIR), lambda b, h, j, pt: (b, j, h)),
            pl.BlockSpec((PAIR, 1), lambda b, h, j, pt: (0, 0)),
            pl.BlockSpec((ns, lq, d), whole, **once),
            pl.BlockSpec((ns, lq, d), whole, **once),
            pl.BlockSpec((ns, lq, d), whole, **once),
            pl.BlockSpec((ns, lq, d), whole, **once),
            pl.BlockSpec((1, PAIR), lambda b, h, j, pt: (0, 0)),
            pl.BlockSpec(memory_space=pl.ANY),
            pl.BlockSpec(memory_space=pl.ANY),
        ],
        out_specs=[
            pl.BlockSpec((1, t, hps * PAIR), lambda b, h, j, pt: (b, j, h)),
            pl.BlockSpec((ns, lq, d), whole),
        ],
        scratch_shapes=[
            pltpu.VMEM((hps, PAIR, 2 * t), F32),
            pltpu.VMEM((2, pages, page_rows, LANES), F32),
            pltpu.VMEM((2, pages * page_rows, LANES), F32),
            pltpu.SemaphoreType.DMA((2, 2)),
            pltpu.VMEM((nrow, d), BF16),
            pltpu.VMEM((nrow, 1), F32),
            pltpu.VMEM((nrow, 1), F32),
            pltpu.VMEM((N_HEADS, 2 * lq, PAIR), F32),
            pltpu.SMEM((1,), jnp.int32),
        ],
    )
    return pl.pallas_call(
        functools.partial(_attn_fused_kernel, t=t, cv=cv, hps=hps, pages=pages, nc=nc, page_rows=page_rows,
                          total=ns * nc),
        grid_spec=grid_spec,
        out_shape=[jax.ShapeDtypeStruct((nb, L, nh * PAIR), BF16), jax.ShapeDtypeStruct((ns, lq, d), F32)],
        compiler_params=pltpu.CompilerParams(dimension_semantics=("arbitrary", "arbitrary", "arbitrary"),
                                             vmem_limit_bytes=VMEM_LIMIT),
        name="attn_fused",
    )(page_table, lam, qp, kp, vt, sza, subln_col, q3, knew3, vnew3, sza3, subln_row, ck2, cv2)


def _divmod(g, n):
    if isinstance(g, int):
        return divmod(g, n)
    return lax.div(g, jnp.int32(n)), lax.rem(g, jnp.int32(n))


def _decode_copies(refs, g, slot, *, pages, nc, page_rows):
    pt_ref, ck_hbm, cv_hbm, kbuf, vbuf, sem = refs
    seq, chunk = _divmod(g, nc)
    cps = []
    for p in range(pages):
        row0 = pl.multiple_of(pt_ref[seq, chunk * pages + p] * page_rows, page_rows)
        cps.append(pltpu.make_async_copy(ck_hbm.at[pl.ds(row0, page_rows), :], kbuf.at[slot, p], sem.at[0, slot]))
        cps.append(pltpu.make_async_copy(cv_hbm.at[pl.ds(row0, page_rows), :],
                                         vbuf.at[slot, pl.ds(p * page_rows, page_rows), :], sem.at[1, slot]))
    return cps


def _decode_parts(g, dma_refs, lam_ref, q_ref, knew_ref, vnew_ref, sza_ref, subln_ref, out_ref,
                  qbd_ref, m_ref, l_ref, acc_ref, *, pages, nc, page_rows, total):
    kbuf, vbuf = dma_refs[3], dma_refs[4]
    lq, d = q_ref.shape[1], q_ref.shape[2]
    nrow = 2 * N_HEADS * lq
    tc = pages * LANES
    active = g < total
    gc = jnp.minimum(g, total - 1)
    seq, c_id = _divmod(gc, nc)
    slot = lax.rem(gc, 2)
    copies = functools.partial(_decode_copies, dma_refs, pages=pages, nc=nc, page_rows=page_rows)

    def pre():
        @pl.when(g + 1 < total)
        def _():
            for cp in copies(g + 1, 1 - slot):
                cp.start()

        @pl.when(active & (c_id == 0))
        def _():
            qt = jnp.concatenate([q_ref[seq]] * (2 * N_HEADS), axis=0)
            rh = lax.broadcasted_iota(jnp.int32, (nrow, d), 0) // lq
            ch = lax.broadcasted_iota(jnp.int32, (nrow, d), 1) // HEAD_DIM
            qbd_ref[...] = jnp.where(rh == ch, qt, 0.0).astype(BF16)
            m_ref[...] = jnp.full_like(m_ref, -jnp.inf)
            l_ref[...] = jnp.zeros_like(l_ref)
            acc_ref[...] = jnp.zeros_like(acc_ref)

        @pl.when(active)
        def _():
            for cp in copies(g, slot):
                cp.wait()

    def softmax(s):
        m_prev = m_ref[...]
        m_new = jnp.maximum(m_prev, jnp.max(s, axis=1, keepdims=True))
        alpha = jnp.exp(m_prev - m_new)
        p = jnp.exp(s - m_new)
        l_ref[...] = alpha * l_ref[...] + jnp.sum(p, axis=1, keepdims=True)
        m_ref[...] = m_new
        return p.astype(BF16), alpha

    def accumulate(pb, alpha, values, heads):
        pv = jnp.dot(pb, values, preferred_element_type=F32)
        for i, h in enumerate(heads):
            rows = slice(2 * lq * h, 2 * lq * (h + 1))
            acc_ref[h] = alpha[rows, :] * acc_ref[h] + pv[rows, i * PAIR:(i + 1) * PAIR]

    def update(s, values):
        pb, alpha = softmax(s)
        accumulate(pb, alpha, values, range(N_HEADS))

    def scores():
        kt = jnp.concatenate([kbuf[slot, p] for p in range(pages)], axis=1).astype(BF16)
        return jnp.dot(qbd_ref[...], kt, preferred_element_type=F32)

    def cached_values(heads):
        return jnp.concatenate(
            [vbuf[slot, pl.ds(h, tc, stride=N_HEADS), :].astype(BF16) for h in heads], axis=1)

    def update_cached(pieces=1):
        state = {}
        per = N_HEADS // pieces

        def score_piece():
            state["s"] = scores()

        def piece(i):
            if i == 0:
                state["pb"], state["alpha"] = softmax(state["s"])
            heads = range(i * per, (i + 1) * per)
            accumulate(state["pb"], state["alpha"], cached_values(heads), heads)

        return [score_piece] + [functools.partial(piece, i) for i in range(pieces)]

    def post():
        pl.when(active & (c_id == nc - 1))(finish)

    def finish():
        pad = jnp.zeros((LANES - lq, d), F32)
        kn = jnp.concatenate([knew_ref[seq], pad], axis=0).astype(BF16)
        vn = jnp.concatenate([vnew_ref[seq], pad], axis=0).astype(BF16)
        s2 = lax.dot_general(qbd_ref[...], kn, (((1,), (1,)), ((), ())), preferred_element_type=F32)
        qi = lax.rem(lax.broadcasted_iota(jnp.int32, (nrow, LANES), 0), lq)
        kj = lax.broadcasted_iota(jnp.int32, (nrow, LANES), 1)
        s2 = jnp.where(kj <= qi, s2, -jnp.inf)
        update(s2, vn)
        inv_l = 1.0 / l_ref[...]
        outs = []
        for h in range(N_HEADS):
            rows = slice(2 * lq * h, 2 * lq * (h + 1))
            o = acc_ref[h] * inv_l[rows, :]
            out = o[:lq] - lam_ref[0, 0] * o[lq:]
            ms = jnp.mean(out * out, axis=1, keepdims=True)
            outs.append(out * lax.rsqrt(ms + NORM_EPS) * subln_ref[...] * (1.0 - LAM_INIT))
        out_ref[seq] = jnp.concatenate(outs, axis=1) * sza_ref[seq]

    return pre, scores, update_cached, post


def _finish_kernel(a_ref, sga_ref, mp_ref, x_ref, wab_ref, wo_ref, g_ref, b_ref, y_ref):
    br_a = jnp.dot(a_ref[...].astype(BF16), wab_ref[...], preferred_element_type=F32)
    m = mp_ref[...] + sga_ref[...].astype(F32) * br_a
    out = jnp.dot(m.astype(BF16), wo_ref[...], preferred_element_type=F32)
    h = ALPHA * x_ref[...] + out
    mu = jnp.mean(h, axis=-1, keepdims=True)
    hc = h - mu
    var = jnp.mean(hc * hc, axis=-1, keepdims=True)
    y_ref[...] = hc * lax.rsqrt(var + NORM_EPS) * g_ref[...] + b_ref[...]


def _finish(a, sga, mp, x, wab, wo, ln_g, ln_b):
    rows, d = x.shape
    tm = min(FIN_TM, rows)
    row = lambda i: (i, 0)
    const = lambda i: (0, 0)
    return pl.pallas_call(
        _finish_kernel,
        grid=(rows // tm,),
        in_specs=[pl.BlockSpec((tm, d), row), pl.BlockSpec((tm, d), row), pl.BlockSpec((tm, d), row),
                  pl.BlockSpec((tm, d), row), pl.BlockSpec((d, d), const), pl.BlockSpec((d, d), const),
                  pl.BlockSpec((1, d), const), pl.BlockSpec((1, d), const)],
        out_specs=pl.BlockSpec((tm, d), row),
        out_shape=jax.ShapeDtypeStruct((rows, d), F32),
        compiler_params=pltpu.CompilerParams(dimension_semantics=("arbitrary",), vmem_limit_bytes=VMEM_LIMIT),
        name="finish",
    )(a, sga, mp, x, wab, wo, ln_g, ln_b)


def kernel(x_prompt, x_sample, cache_k, cache_v, state_pool, page_table, w_in, w_pool_mix, pool_scale, lambda_q1, lambda_k1, lambda_q2, lambda_k2, subln_w, w_pool_branch, w_attn_branch, w_o, ln_g, ln_b):
    assert w_in.shape[0] == DEPTH
    nb, L, d = x_prompt.shape
    ns, ls, _ = x_sample.shape
    n_phys, page_size = cache_k.shape[1], cache_k.shape[2]
    assert page_size == LANES
    past_len = page_table.shape[1] * page_size

    lam = (jnp.exp(jnp.sum(lambda_q1[0] * lambda_k1[0])) - jnp.exp(jnp.sum(lambda_q2[0] * lambda_k2[0]))
           + LAM_INIT).reshape(1, 1).astype(F32)
    w_in_b = w_in[0].astype(BF16)
    wmix_b = w_pool_mix[0].astype(BF16)
    wpb_b = w_pool_branch[0].astype(BF16)
    wab_b = w_attn_branch[0].astype(BF16)
    wo_b = w_o[0].astype(BF16)
    pscale = pool_scale[0].reshape(1, POOL_WIDTH)
    g_row = ln_g[0].reshape(1, d)
    b_row = ln_b[0].reshape(1, d)

    tabs_p = _rope_tables(jnp.arange(L))
    k_p, v_p, qp, kp, vt, sza, sga, mp, tail = _proj_prompt(x_prompt, tabs_p, w_in_b, wmix_b, pscale, wpb_b)
    pos_s = jnp.tile(past_len + jnp.arange(ls), ns)
    tabs_s = _rope_tables(pos_s)
    prev16 = jnp.pad(state_pool[0], ((0, 0), (16 - POOL_CTX, 0), (0, 0)))
    x_s2 = x_sample.reshape(ns * ls, d)
    k_s, v_s, q_s, sza_s, sga_s, mp_s, ext_s = _proj_sample(x_s2, tabs_s, prev16, w_in_b, wmix_b, pscale, wpb_b,
                                                            ns, ls, past_len)
    page_rows = page_size * d // LANES
    ck2 = jnp.transpose(cache_k[0], (0, 2, 3, 1)).reshape(n_phys * page_rows, LANES)
    cv2 = cache_v[0].reshape(n_phys * page_rows, LANES)
    r3 = lambda z: z.reshape(ns, ls, d)
    a_p, a_s = _attn_fused(page_table, lam, qp, kp, vt, sza, subln_w[0].reshape(PAIR, 1),
                           r3(q_s), r3(k_s), r3(v_s), r3(sza_s), subln_w[0].reshape(1, PAIR), ck2, cv2, page_rows)
    y_p = _finish(a_p.reshape(nb * L, d), sga.reshape(nb * L, d), mp.reshape(nb * L, d),
                  x_prompt.reshape(nb * L, d), wab_b, wo_b, g_row, b_row).reshape(nb, L, d)
    y_s = _finish(a_s.reshape(ns * ls, d), sga_s, mp_s, x_s2, wab_b, wo_b, g_row, b_row).reshape(ns, ls, d)

    return (y_p, y_s,
            k_p.reshape(1, nb, L, 2 * N_HEADS, HEAD_DIM), v_p.reshape(1, nb, L, N_HEADS, PAIR),
            tail[:, 16 - POOL_CTX:][None],
            k_s.reshape(1, ns, ls, 2 * N_HEADS, HEAD_DIM), v_s.reshape(1, ns, ls, N_HEADS, PAIR),
            ext_s[:, 16 + ls - POOL_CTX:][None])
```

```python
import functools
import math

import jax
import jax.numpy as jnp
from jax import lax
from jax.experimental import pallas as pl
from jax.experimental.pallas import tpu as pltpu

F32 = jnp.float32
BF16 = jnp.bfloat16

N_HEADS = 8
HEAD_DIM = 64
PAIR = 2 * HEAD_DIM
VT_ROWS = PAIR + 16
POOL_WINDOWS = (2, 4, 8, 16)
POOL_GC = 128
POOL_WIDTH = POOL_GC * len(POOL_WINDOWS)
POOL_CTX = max(POOL_WINDOWS) - 1
ROPE_THETA = 10000.0
NORM_EPS = 1e-5
DEPTH = 1
ALPHA = (2.0 * DEPTH) ** 0.25
LAM_INIT = 0.8 - 0.6 * math.exp(-0.3 * 0)
LOG2E = math.log2(math.e)

C_UP, C_ZP, C_Q, C_K, C_V, C_ZA, C_GP, C_GA, C_END = (0, 512, 1024, 2048, 3072, 4096, 5120, 6144, 7168)

LANES = 128
POOL_HIST = 32
VMEM_LIMIT = 56 * 1024 * 1024

PROJ_TM = 256
ATTN_T = 512
ATTN_LOOKAHEAD = 1
ATTN_HPS = 4
FIN_TM = 512
PAGES_PER_CHUNK = 8


def _silu(z):
    return z * jax.nn.sigmoid(z)


def _rope_block(x, c, sn, sp):
    return x * c + pltpu.roll(x, LANES - 32, axis=1) * sn + pltpu.roll(x, 32, axis=1) * sp


def _rope_tables(pos):
    inv_freq = 1.0 / (ROPE_THETA ** (jnp.arange(0, HEAD_DIM, 2, dtype=F32) / HEAD_DIM))
    ang = pos.astype(F32)[:, None] * inv_freq[None, :]
    cos = jnp.cos(ang)
    sin = jnp.sin(ang)
    z = jnp.zeros_like(sin)
    c = jnp.tile(cos, (1, 4))
    sn = jnp.tile(jnp.concatenate([-sin, z], axis=1), (1, 2))
    sp = jnp.tile(jnp.concatenate([z, sin], axis=1), (1, 2))
    return c, sn, sp


def _pool_mix(sums, u, pos, wmix_ref, pscale_ref):
    outs = []
    for g, w in enumerate(POOL_WINDOWS):
        cols = slice(g * POOL_GC, (g + 1) * POOL_GC)
        cnt = jnp.minimum(pos + 1, w).astype(F32)
        d = sums[g] / cnt - u[:, cols]
        mixed = jnp.dot(d.astype(BF16), wmix_ref[g], preferred_element_type=F32)
        outs.append(mixed * pscale_ref[:, cols])
    return jnp.concatenate(outs, axis=1)


def _proj_prompt_kernel(x_ref, c_ref, sn_ref, sp_ref, w_ref, wmix_ref, pscale_ref, wpb_ref,
                        k_ref, v_ref, qp_ref, kp_ref, vt_ref, sza_ref, sga_ref, mp_ref, tail_ref,
                        ext_ref, sb_ref, *, tm):
    t = pl.program_id(1)
    n = POOL_HIST + tm

    @pl.when(t == 0)
    def _():
        ext_ref[0:POOL_HIST, :] = jnp.zeros((POOL_HIST, POOL_WIDTH), F32)

    xb = x_ref[0].astype(BF16)

    def proj(lo, hi):
        return jnp.dot(xb, w_ref[:, lo:hi], preferred_element_type=F32)

    u = proj(C_UP, C_ZP)
    ext_ref[POOL_HIST:n, :] = u
    sb_ref[0, 8:n, :] = ext_ref[8:n, :] + ext_ref[7:n - 1, :]
    sb_ref[1, 16:n, 128:] = sb_ref[0, 16:n, 128:] + sb_ref[0, 14:n - 2, 128:]
    sb_ref[2, 24:n, 256:] = sb_ref[1, 24:n, 256:] + sb_ref[1, 20:n - 4, 256:]
    s16 = sb_ref[2, 32:n, 384:] + sb_ref[2, 24:n - 8, 384:]
    sums = [sb_ref[0, 32:n, 0:128], sb_ref[1, 32:n, 128:256], sb_ref[2, 32:n, 256:384], s16]
    tail_ref[0] = ext_ref[n - 16:n, :]
    ext_ref[0:POOL_HIST, :] = ext_ref[tm:n, :]

    c = c_ref[...]
    sn = sn_ref[...]
    sp = sp_ref[...]
    q = proj(C_Q, C_K) * (HEAD_DIM ** -0.5 * LOG2E)
    for h in range(N_HEADS):
        cols = slice(h * PAIR, (h + 1) * PAIR)
        qp_ref[0, h] = _rope_block(q[:, cols], c, sn, sp).astype(BF16)
    k = proj(C_K, C_V)
    for h in range(N_HEADS):
        cols = slice(h * PAIR, (h + 1) * PAIR)
        kr = _rope_block(k[:, cols], c, sn, sp)
        k_ref[0, :, cols] = kr
        kp_ref[0, h] = kr.astype(BF16)
    v = proj(C_V, C_ZA)
    v_ref[0] = v
    for h in range(N_HEADS):
        vt_ref[0, h, 0, 0:PAIR, :] = v[:, h * PAIR:(h + 1) * PAIR].T.astype(BF16)
        vt_ref[0, h, 0, PAIR:VT_ROWS, :] = jnp.ones((VT_ROWS - PAIR, tm), BF16)
    sza_ref[0] = _silu(proj(C_ZA, C_GP)).astype(BF16)
    sga_ref[0] = jax.nn.sigmoid(proj(C_GA, C_END)).astype(BF16)

    pos = t * tm + lax.broadcasted_iota(jnp.int32, (tm, POOL_GC), 0)
    pool_o = _pool_mix(sums, u, pos, wmix_ref, pscale_ref)
    zp = proj(C_ZP, C_Q)
    br_p = jnp.dot((pool_o * _silu(zp)).astype(BF16), wpb_ref[...], preferred_element_type=F32)
    mp_ref[0] = (jax.nn.sigmoid(proj(C_GP, C_GA)) * br_p).astype(BF16)


def _proj_prompt(x, tabs, w_in, wmix, pscale, wpb):
    nb, L, d = x.shape
    tm = PROJ_TM
    nt = L // tm
    const = dict(pipeline_mode=pl.Buffered(1))
    row = lambda b, t: (b, t, 0)
    head = lambda b, t: (b, 0, t, 0)
    in_specs = [
        pl.BlockSpec((1, tm, d), row),
        pl.BlockSpec((tm, LANES), lambda b, t: (t, 0)),
        pl.BlockSpec((tm, LANES), lambda b, t: (t, 0)),
        pl.BlockSpec((tm, LANES), lambda b, t: (t, 0)),
        pl.BlockSpec((d, C_END), lambda b, t: (0, 0), **const),
        pl.BlockSpec((len(POOL_WINDOWS), POOL_GC, POOL_GC), lambda b, t: (0, 0, 0), **const),
        pl.BlockSpec((1, POOL_WIDTH), lambda b, t: (0, 0), **const),
        pl.BlockSpec((POOL_WIDTH, d), lambda b, t: (0, 0), **const),
    ]
    out_specs = [
        pl.BlockSpec((1, tm, d), row),
        pl.BlockSpec((1, tm, d), row),
        pl.BlockSpec((1, N_HEADS, tm, PAIR), head),
        pl.BlockSpec((1, N_HEADS, tm, PAIR), head),
        pl.BlockSpec((1, N_HEADS, 1, VT_ROWS, tm), lambda b, t: (b, 0, t, 0, 0)),
        pl.BlockSpec((1, tm, d), row),
        pl.BlockSpec((1, tm, d), row),
        pl.BlockSpec((1, tm, d), row),
        pl.BlockSpec((1, 16, POOL_WIDTH), lambda b, t: (b, 0, 0)),
    ]
    out_shape = [
        jax.ShapeDtypeStruct((nb, L, d), F32),
        jax.ShapeDtypeStruct((nb, L, d), F32),
        jax.ShapeDtypeStruct((nb, N_HEADS, L, PAIR), BF16),
        jax.ShapeDtypeStruct((nb, N_HEADS, L, PAIR), BF16),
        jax.ShapeDtypeStruct((nb, N_HEADS, nt, VT_ROWS, tm), BF16),
        jax.ShapeDtypeStruct((nb, L, d), BF16),
        jax.ShapeDtypeStruct((nb, L, d), BF16),
        jax.ShapeDtypeStruct((nb, L, d), BF16),
        jax.ShapeDtypeStruct((nb, 16, POOL_WIDTH), F32),
    ]
    return pl.pallas_call(
        functools.partial(_proj_prompt_kernel, tm=tm),
        grid=(nb, nt),
        in_specs=in_specs,
        out_specs=out_specs,
        out_shape=out_shape,
        scratch_shapes=[pltpu.VMEM((POOL_HIST + tm, POOL_WIDTH), F32),
                        pltpu.VMEM((3, POOL_HIST + tm, POOL_WIDTH), F32)],
        compiler_params=pltpu.CompilerParams(dimension_semantics=("arbitrary", "arbitrary"),
                                             vmem_limit_bytes=VMEM_LIMIT),
        name="proj_prompt",
    )(x, *tabs, w_in, wmix, pscale, wpb)


def _proj_sample_kernel(x_ref, c_ref, sn_ref, sp_ref, prev_ref, w_ref, wmix_ref, pscale_ref, wpb_ref,
                        k_ref, v_ref, q_ref, sza_ref, sga_ref, mp_ref, ext_ref, *, ns, ls, pos0):
    rows = ns * ls
    xb = x_ref[...].astype(BF16)

    def proj(lo, hi):
        return jnp.dot(xb, w_ref[:, lo:hi], preferred_element_type=F32)

    u = proj(C_UP, C_ZP)
    ext_ref[:, 0:16, :] = prev_ref[...]
    ext_ref[:, 16:16 + ls, :] = u.reshape(ns, ls, POOL_WIDTH)
    sums = []
    for g, w in enumerate(POOL_WINDOWS):
        cols = slice(g * POOL_GC, (g + 1) * POOL_GC)
        s = ext_ref[:, 16:16 + ls, cols]
        for j in range(1, w):
            s = s + ext_ref[:, 16 - j:16 + ls - j, cols]
        sums.append(s.reshape(rows, POOL_GC))
    pos = pos0 + lax.broadcasted_iota(jnp.int32, (ns, ls, POOL_GC), 1).reshape(rows, POOL_GC)
    pool_o = _pool_mix(sums, u, pos, wmix_ref, pscale_ref)
    zp = proj(C_ZP, C_Q)
    br_p = jnp.dot((pool_o * _silu(zp)).astype(BF16), wpb_ref[...], preferred_element_type=F32)
    mp_ref[...] = jax.nn.sigmoid(proj(C_GP, C_GA)) * br_p

    c = c_ref[...]
    sn = sn_ref[...]
    sp = sp_ref[...]
    q = proj(C_Q, C_K) * (HEAD_DIM ** -0.5)
    k = proj(C_K, C_V)
    for h in range(N_HEADS):
        cols = slice(h * PAIR, (h + 1) * PAIR)
        q_ref[:, cols] = _rope_block(q[:, cols], c, sn, sp)
        k_ref[:, cols] = _rope_block(k[:, cols], c, sn, sp)
    v_ref[...] = proj(C_V, C_ZA)
    sza_ref[...] = _silu(proj(C_ZA, C_GP))
    sga_ref[...] = jax.nn.sigmoid(proj(C_GA, C_END))


def _proj_sample(x2, tabs, prev16, w_in, wmix, pscale, wpb, ns, ls, pos0):
    rows, d = x2.shape
    big = jax.ShapeDtypeStruct((rows, d), F32)
    return pl.pallas_call(
        functools.partial(_proj_sample_kernel, ns=ns, ls=ls, pos0=pos0),
        out_shape=[big, big, big, big, big, big, jax.ShapeDtypeStruct((ns, 16 + ls, POOL_WIDTH), F32)],
        compiler_params=pltpu.CompilerParams(vmem_limit_bytes=VMEM_LIMIT),
        name="proj_sample",
    )(x2, *tabs, prev16, w_in, wmix, pscale, wpb)


COL_REDUCE_ROWS = 64


def _col_reduce(x, op):
    rows, cols = x.shape
    slabs = op(x.reshape(rows // COL_REDUCE_ROWS, COL_REDUCE_ROWS, cols), axis=0)
    return op(slabs, axis=0, keepdims=True)


def _attn_units(kts, vts, ws, ms, ls, acc_ref, mask, between=None):
    between = between or {}

    def scores(u):
        s = lax.dot_general(kts[u], ws[u], (((1,), (1,)), ((), ())), preferred_element_type=F32)
        between.get(("scores", u), lambda: None)()
        return s

    n = len(ws)
    ms_new, ls_new = [], []
    pending = [scores(u) for u in range(min(ATTN_LOOKAHEAD, n))]
    for u in range(n):
        s = pending.pop(0)
        if u + ATTN_LOOKAHEAD < n:
            pending.append(scores(u + ATTN_LOOKAHEAD))
        if mask is not None:
            s = jnp.where(mask, s, -jnp.inf)
        m_new = jnp.maximum(ms[u], _col_reduce(s, jnp.max))
        alpha = jnp.exp2(ms[u] - m_new)
        p = jnp.exp2(s - m_new)
        ms_new.append(m_new)
        pv = jnp.dot(vts[u], p.astype(BF16), preferred_element_type=F32)
        acc_ref[u] = alpha * acc_ref[u] + pv[:PAIR]
        ls_new.append(alpha * ls[u] + pv[PAIR:PAIR + 1])
        between.get(("unit", u), lambda: None)()
    return tuple(ms_new), tuple(ls_new)


def _attn_fused_kernel(pt_ref, lam_ref, q_ref, k_ref, vt_ref, sza_ref, subln_ref,
                       qs_ref, knew_ref, vnew_ref, szas_ref, sublnr_ref, ck_hbm, cv_hbm,
                       out_ref, dout_ref,
                       acc_ref, kbuf, vbuf, sem, qbd_ref, dm_ref, dl_ref, dacc_ref, gcount_ref,
                       *, t, cv, hps, pages, nc, page_rows, total):
    j = pl.program_id(2)
    lin = (pl.program_id(0) * pl.num_programs(1) + pl.program_id(1)) * pl.num_programs(2) + j
    n_lin = pl.num_programs(0) * pl.num_programs(1) * pl.num_programs(2)
    dma_refs = (pt_ref, ck_hbm, cv_hbm, kbuf, vbuf, sem)
    dec_kw = dict(pages=pages, nc=nc, page_rows=page_rows)

    @pl.when(lin == 0)
    def _():
        gcount_ref[0] = 0
        for cp in _decode_copies(dma_refs, 0, 0, **dec_kw):
            cp.start()

    g0 = gcount_ref[0]

    def decode_parts(g):
        return _decode_parts(g, dma_refs, lam_ref, qs_ref, knew_ref, vnew_ref, szas_ref, sublnr_ref, dout_ref,
                             qbd_ref, dm_ref, dl_ref, dacc_ref, total=total, **dec_kw)

    lane = lax.broadcasted_iota(jnp.int32, (t, PAIR), 1)
    ws = []
    for u in range(hps):
        q = q_ref[0, u]
        zero = jnp.zeros_like(q)
        ws.append(jnp.concatenate([jnp.where(lane < HEAD_DIM, q, zero), jnp.where(lane >= HEAD_DIM, q, zero)],
                                  axis=0))
    acc_ref[...] = jnp.zeros_like(acc_ref)
    nck = t // cv

    def tile(i, ms, ls, mask):
        pre, dec_scores, dec_update, post = decode_parts(g0 + i)
        pre()
        kts = [k_ref[0, u, pl.ds(pl.multiple_of(i * t, t), t), :] for u in range(hps)]
        vts = [jnp.concatenate([vt_ref[0, u, i * nck + c] for c in range(nck)], axis=1) for u in range(hps)]
        dec_pieces = dec_update(pieces=hps)
        dec_pieces[0]()
        between = {("scores", u): piece for u, piece in enumerate(dec_pieces[1:])}
        ms, ls = _attn_units(kts, vts, ws, ms, ls, acc_ref, mask, between=between)
        post()
        return ms, ls

    m0 = tuple(jnp.full((1, 2 * t), -jnp.inf, F32) for _ in range(hps))
    l0 = tuple(jnp.zeros((1, 2 * t), F32) for _ in range(hps))
    ms, ls = lax.fori_loop(0, j, lambda i, c: tile(i, c[0], c[1], None), (m0, l0))
    krow = lax.broadcasted_iota(jnp.int32, (t, 2 * t), 0)
    qcol = lax.broadcasted_iota(jnp.int32, (t, 2 * t), 1)
    qcol = jnp.where(qcol >= t, qcol - t, qcol)
    ms, ls = tile(j, ms, ls, krow <= qcol)

    for u in range(hps):
        o = acc_ref[u] * (1.0 / ls[u])
        out = o[:, :t] - lam_ref[0, 0] * o[:, t:]
        msq = jnp.mean(out * out, axis=0, keepdims=True)
        out = out * lax.rsqrt(msq + NORM_EPS) * subln_ref[...] * (1.0 - LAM_INIT)
        cols = slice(u * PAIR, (u + 1) * PAIR)
        out_ref[0, :, cols] = (out.T * sza_ref[0, :, cols].astype(F32)).astype(BF16)

    g_next = g0 + j + 1
    gcount_ref[0] = g_next

    @pl.when(lin == n_lin - 1)
    def _():
        def body(g, carry):
            pre, dec_scores, dec_update, post = decode_parts(g)
            pre()
            for piece in dec_update():
                piece()
            post()
            return carry
        lax.fori_loop(g_next, total, body, 0)


def _attn_fused(page_table, lam, qp, kp, vt, sza, subln_col, q3, knew3, vnew3, sza3, subln_row, ck2, cv2,
                page_rows):
    nb, nh, L, _ = qp.shape
    ns, lq, d = q3.shape
    t = ATTN_T
    hps = ATTN_HPS
    nchunk, cv = vt.shape[2], vt.shape[4]
    n_pages = page_table.shape[1]
    pages = PAGES_PER_CHUNK
    assert n_pages % pages == 0
    nc = n_pages // pages
    nrow = 2 * N_HEADS * lq
    once = dict(pipeline_mode=pl.Buffered(1))
    whole = lambda b, h, j, pt: (0, 0, 0)
    grid_spec = pltpu.PrefetchScalarGridSpec(
        num_scalar_prefetch=1,
        grid=(nb, nh // hps, L // t),
        in_specs=[
            pl.BlockSpec(memory_space=pltpu.SMEM),
            pl.BlockSpec((1, hps, t, PAIR), lambda b, h, j, pt: (b, h, j, 0)),
            pl.BlockSpec((1, hps, L, PAIR), lambda b, h, j, pt: (b, h, 0, 0), **once),
            pl.BlockSpec((1, hps, nchunk, VT_ROWS, cv), lambda b, h, j, pt: (b, h, 0, 0, 0), **once),
            pl.BlockSpec((1, t, hps * PAIR), lambda b, h, j, pt: (b, j, h)),
            pl.BlockSpec((PAIR, 1), lambda b, h, j, pt: (0, 0)),
            pl.BlockSpec((ns, lq, d), whole, **once),
            pl.BlockSpec((ns, lq, d), whole, **once),
            pl.BlockSpec((ns, lq, d), whole, **once),
            pl.BlockSpec((ns, lq, d), whole, **once),
            pl.BlockSpec((1, PAIR), lambda b, h, j, pt: (0, 0)),
            pl.BlockSpec(memory_space=pl.ANY),
            pl.BlockSpec(memory_space=pl.ANY),
        ],
        out_specs=[
            pl.BlockSpec((1, t, hps * PAIR), lambda b, h, j, pt: (b, j, h)),
            pl.BlockSpec((ns, lq, d), whole),
        ],
        scratch_shapes=[
            pltpu.VMEM((hps, PAIR, 2 * t), F32),
            pltpu.VMEM((2, pages, page_rows, LANES), F32),
            pltpu.VMEM((2, pages * page_rows, LANES), F32),
            pltpu.SemaphoreType.DMA((2, 2)),
            pltpu.VMEM((nrow, d), BF16),
            pltpu.VMEM((nrow, 1), F32),
            pltpu.VMEM((nrow, 1), F32),
            pltpu.VMEM((N_HEADS, 2 * lq, PAIR), F32),
            pltpu.SMEM((1,), jnp.int32),
        ],
    )
    return pl.pallas_call(
        functools.partial(_attn_fused_kernel, t=t, cv=cv, hps=hps, pages=pages, nc=nc, page_rows=page_rows,
                          total=ns * nc),
        grid_spec=grid_spec,
        out_shape=[jax.ShapeDtypeStruct((nb, L, nh * PAIR), BF16), jax.ShapeDtypeStruct((ns, lq, d), F32)],
        compiler_params=pltpu.CompilerParams(dimension_semantics=("arbitrary", "arbitrary", "arbitrary"),
                                             vmem_limit_bytes=VMEM_LIMIT),
        name="attn_fused",
    )(page_table, lam, qp, kp, vt, sza, subln_col, q3, knew3, vnew3, sza3, subln_row, ck2, cv2)


def _divmod(g, n):
    if isinstance(g, int):
        return divmod(g, n)
    return lax.div(g, jnp.int32(n)), lax.rem(g, jnp.int32(n))


def _decode_copies(refs, g, slot, *, pages, nc, page_rows):
    pt_ref, ck_hbm, cv_hbm, kbuf, vbuf, sem = refs
    seq, chunk = _divmod(g, nc)
    cps = []
    for p in range(pages):
        row0 = pl.multiple_of(pt_ref[seq, chunk * pages + p] * page_rows, page_rows)
        cps.append(pltpu.make_async_copy(ck_hbm.at[pl.ds(row0, page_rows), :], kbuf.at[slot, p], sem.at[0, slot]))
        cps.append(pltpu.make_async_copy(cv_hbm.at[pl.ds(row0, page_rows), :],
                                         vbuf.at[slot, pl.ds(p * page_rows, page_rows), :], sem.at[1, slot]))
    return cps


def _decode_parts(g, dma_refs, lam_ref, q_ref, knew_ref, vnew_ref, sza_ref, subln_ref, out_ref,
                  qbd_ref, m_ref, l_ref, acc_ref, *, pages, nc, page_rows, total):
    kbuf, vbuf = dma_refs[3], dma_refs[4]
    lq, d = q_ref.shape[1], q_ref.shape[2]
    nrow = 2 * N_HEADS * lq
    tc = pages * LANES
    active = g < total
    gc = jnp.minimum(g, total - 1)
    seq, c_id = _divmod(gc, nc)
    slot = lax.rem(gc, 2)
    copies = functools.partial(_decode_copies, dma_refs, pages=pages, nc=nc, page_rows=page_rows)

    def pre():
        @pl.when(g + 1 < total)
        def _():
            for cp in copies(g + 1, 1 - slot):
                cp.start()

        @pl.when(active & (c_id == 0))
        def _():
            qt = jnp.concatenate([q_ref[seq]] * (2 * N_HEADS), axis=0)
            rh = lax.broadcasted_iota(jnp.int32, (nrow, d), 0) // lq
            ch = lax.broadcasted_iota(jnp.int32, (nrow, d), 1) // HEAD_DIM
            qbd_ref[...] = jnp.where(rh == ch, qt, 0.0).astype(BF16)
            m_ref[...] = jnp.full_like(m_ref, -jnp.inf)
            l_ref[...] = jnp.zeros_like(l_ref)
            acc_ref[...] = jnp.zeros_like(acc_ref)

        @pl.when(active)
        def _():
            for cp in copies(g, slot):
                cp.wait()

    def softmax(s):
        m_prev = m_ref[...]
        m_new = jnp.maximum(m_prev, jnp.max(s, axis=1, keepdims=True))
        alpha = jnp.exp(m_prev - m_new)
        p = jnp.exp(s - m_new)
        l_ref[...] = alpha * l_ref[...] + jnp.sum(p, axis=1, keepdims=True)
        m_ref[...] = m_new
        return p.astype(BF16), alpha

    def accumulate(pb, alpha, values, heads):
        pv = jnp.dot(pb, values, preferred_element_type=F32)
        for i, h in enumerate(heads):
            rows = slice(2 * lq * h, 2 * lq * (h + 1))
            acc_ref[h] = alpha[rows, :] * acc_ref[h] + pv[rows, i * PAIR:(i + 1) * PAIR]

    def update(s, values):
        pb, alpha = softmax(s)
        accumulate(pb, alpha, values, range(N_HEADS))

    def scores():
        kt = jnp.concatenate([kbuf[slot, p] for p in range(pages)], axis=1).astype(BF16)
        return jnp.dot(qbd_ref[...], kt, preferred_element_type=F32)

    def cached_values(heads):
        return jnp.concatenate(
            [vbuf[slot, pl.ds(h, tc, stride=N_HEADS), :].astype(BF16) for h in heads], axis=1)

    def update_cached(pieces=1):
        state = {}
        per = N_HEADS // pieces

        def score_piece():
            state["s"] = scores()

        def piece(i):
            if i == 0:
                state["pb"], state["alpha"] = softmax(state["s"])
            heads = range(i * per, (i + 1) * per)
            accumulate(state["pb"], state["alpha"], cached_values(heads), heads)

        return [score_piece] + [functools.partial(piece, i) for i in range(pieces)]

    def post():
        pl.when(active & (c_id == nc - 1))(finish)

    def finish():
        pad = jnp.zeros((LANES - lq, d), F32)
        kn = jnp.concatenate([knew_ref[seq], pad], axis=0).astype(BF16)
        vn = jnp.concatenate([vnew_ref[seq], pad], axis=0).astype(BF16)
        s2 = lax.dot_general(qbd_ref[...], kn, (((1,), (1,)), ((), ())), preferred_element_type=F32)
        qi = lax.rem(lax.broadcasted_iota(jnp.int32, (nrow, LANES), 0), lq)
        kj = lax.broadcasted_iota(jnp.int32, (nrow, LANES), 1)
        s2 = jnp.where(kj <= qi, s2, -jnp.inf)
        update(s2, vn)
        inv_l = 1.0 / l_ref[...]
        outs = []
        for h in range(N_HEADS):
            rows = slice(2 * lq * h, 2 * lq * (h + 1))
            o = acc_ref[h] * inv_l[rows, :]
            out = o[:lq] - lam_ref[0, 0] * o[lq:]
            ms = jnp.mean(out * out, axis=1, keepdims=True)
            outs.append(out * lax.rsqrt(ms + NORM_EPS) * subln_ref[...] * (1.0 - LAM_INIT))
        out_ref[seq] = jnp.concatenate(outs, axis=1) * sza_ref[seq]

    return pre, scores, update_cached, post


def _finish_kernel(a_ref, sga_ref, mp_ref, x_ref, wab_ref, wo_ref, g_ref, b_ref, y_ref):
    br_a = jnp.dot(a_ref[...].astype(BF16), wab_ref[...], preferred_element_type=F32)
    m = mp_ref[...] + sga_ref[...].astype(F32) * br_a
    out = jnp.dot(m.astype(BF16), wo_ref[...], preferred_element_type=F32)
    h = ALPHA * x_ref[...] + out
    mu = jnp.mean(h, axis=-1, keepdims=True)
    hc = h - mu
    var = jnp.mean(hc * hc, axis=-1, keepdims=True)
    y_ref[...] = hc * lax.rsqrt(var + NORM_EPS) * g_ref[...] + b_ref[...]


def _finish(a, sga, mp, x, wab, wo, ln_g, ln_b):
    rows, d = x.shape
    tm = min(FIN_TM, rows)
    row = lambda i: (i, 0)
    const = lambda i: (0, 0)
    return pl.pallas_call(
        _finish_kernel,
        grid=(rows // tm,),
        in_specs=[pl.BlockSpec((tm, d), row), pl.BlockSpec((tm, d), row), pl.BlockSpec((tm, d), row),
                  pl.BlockSpec((tm, d), row), pl.BlockSpec((d, d), const), pl.BlockSpec((d, d), const),
                  pl.BlockSpec((1, d), const), pl.BlockSpec((1, d), const)],
        out_specs=pl.BlockSpec((tm, d), row),
        out_shape=jax.ShapeDtypeStruct((rows, d), F32),
        compiler_params=pltpu.CompilerParams(dimension_semantics=("arbitrary",), vmem_limit_bytes=VMEM_LIMIT),
        name="finish",
    )(a, sga, mp, x, wab, wo, ln_g, ln_b)


def kernel(x_prompt, x_sample, cache_k, cache_v, state_pool, page_table, w_in, w_pool_mix, pool_scale, lambda_q1, lambda_k1, lambda_q2, lambda_k2, subln_w, w_pool_branch, w_attn_branch, w_o, ln_g, ln_b):
    assert w_in.shape[0] == DEPTH
    nb, L, d = x_prompt.shape
    ns, ls, _ = x_sample.shape
    n_phys, page_size = cache_k.shape[1], cache_k.shape[2]
    assert page_size == LANES
    past_len = page_table.shape[1] * page_size

    lam = (jnp.exp(jnp.sum(lambda_q1[0] * lambda_k1[0])) - jnp.exp(jnp.sum(lambda_q2[0] * lambda_k2[0]))
           + LAM_INIT).reshape(1, 1).astype(F32)
    w_in_b = w_in[0].astype(BF16)
    wmix_b = w_pool_mix[0].astype(BF16)
    wpb_b = w_pool_branch[0].astype(BF16)
    wab_b = w_attn_branch[0].astype(BF16)
    wo_b = w_o[0].astype(BF16)
    pscale = pool_scale[0].reshape(1, POOL_WIDTH)
    g_row = ln_g[0].reshape(1, d)
    b_row = ln_b[0].reshape(1, d)

    tabs_p = _rope_tables(jnp.arange(L))
    k_p, v_p, qp, kp, vt, sza, sga, mp, tail = _proj_prompt(x_prompt, tabs_p, w_in_b, wmix_b, pscale, wpb_b)
    pos_s = jnp.tile(past_len + jnp.arange(ls), ns)
    tabs_s = _rope_tables(pos_s)
    prev16 = jnp.pad(state_pool[0], ((0, 0), (16 - POOL_CTX, 0), (0, 0)))
    x_s2 = x_sample.reshape(ns * ls, d)
    k_s, v_s, q_s, sza_s, sga_s, mp_s, ext_s = _proj_sample(x_s2, tabs_s, prev16, w_in_b, wmix_b, pscale, wpb_b,
                                                            ns, ls, past_len)
    page_rows = page_size * d // LANES
    ck2 = jnp.transpose(cache_k[0], (0, 2, 3, 1)).reshape(n_phys * page_rows, LANES)
    cv2 = cache_v[0].reshape(n_phys * page_rows, LANES)
    r3 = lambda z: z.reshape(ns, ls, d)
    a_p, a_s = _attn_fused(page_table, lam, qp, kp, vt, sza, subln_w[0].reshape(PAIR, 1),
                           r3(q_s), r3(k_s), r3(v_s), r3(sza_s), subln_w[0].reshape(1, PAIR), ck2, cv2, page_rows)
    y_p = _finish(a_p.reshape(nb * L, d), sga.reshape(nb * L, d), mp.reshape(nb * L, d),
                  x_prompt.reshape(nb * L, d), wab_b, wo_b, g_row, b_row).reshape(nb, L, d)
    y_s = _finish(a_s.reshape(ns * ls, d), sga_s, mp_s, x_s2, wab_b, wo_b, g_row, b_row).reshape(ns, ls, d)

    return (y_p, y_s,
            k_p.reshape(1, nb, L, 2 * N_HEADS, HEAD_DIM), v_p.reshape(1, nb, L, N_HEADS, PAIR),
            tail[:, 16 - POOL_CTX:][None],
            k_s.reshape(1, ns, ls, 2 * N_HEADS, HEAD_DIM), v_s.reshape(1, ns, ls, N_HEADS, PAIR),
            ext_s[:, 16 + ls - POOL_CTX:][None])
```
